```python
import jax, jax.numpy as jnp
from jax import lax
import numpy as np


D_MODEL = 1024
BATCH = 8
SEQ = 2048
DEPTH = 2
DEC_BATCH = 32
DEC_SEQ = 2048
PAST_LEN = 128

GRID_W = 64
EPS = 1e-6
MLA_HEADS = 8
QK_NOPE = 64
QK_ROPE = 32
V_DIM = 64
Q_LORA = 256
KV_LORA = 256
ROPE_THETA = 10000.0
ATTN_BLOCK = 128
MLA_WIDTH = MLA_HEADS * V_DIM
NA_HEADS = 8
NA_DIM = 64
NA_WIN_H = 8
NA_WIN_W = 16
NA_WIDTH = NA_HEADS * NA_DIM
MIX0_WIDTH = MLA_WIDTH + NA_WIDTH
IN0_SIZES = (Q_LORA, KV_LORA, QK_ROPE, NA_WIDTH, NA_WIDTH, NA_WIDTH, MIX0_WIDTH)
IN0_WIDTH = sum(IN0_SIZES)
ML_HEADS = 4
ML_INNER = 2 * D_MODEL
ML_DIM = ML_INNER // ML_HEADS
ML_CONV = 5
QKV_BLOCK = 4
QKV_NB = ML_INNER // QKV_BLOCK
ML_CHUNK = 128
N_EVEN = (DEPTH + 1) // 2
N_ODD = DEPTH // 2

kernel_name = 'hybrid_mla_natten_mlstm_encoder'


def rmsnorm(x, g):
    xf = x.astype(jnp.float32)
    y = xf * lax.rsqrt(jnp.mean(xf * xf, axis=-1, keepdims=True) + EPS)
    return (y * g.astype(jnp.float32)).astype(x.dtype)


def rope_tables(S):
    pos = jnp.arange(S, dtype=jnp.float32)
    inv = 1.0 / (ROPE_THETA ** (jnp.arange(0, QK_ROPE, 2, dtype=jnp.float32) / QK_ROPE))
    ang = pos[:, None] * inv[None, :]
    return jnp.cos(ang), jnp.sin(ang)


def apply_rope(x, cos, sin):
    half = x.shape[-1] // 2
    x1, x2 = x[..., :half], x[..., half:]
    c = cos[None, :, None, :].astype(x.dtype)
    s = sin[None, :, None, :].astype(x.dtype)
    return jnp.concatenate([x1 * c - x2 * s, x1 * s + x2 * c], axis=-1)


def mla(q_lat, kv_lat, k_rope, g_q, g_kv, w_uq, w_ukv):
    B, S, _ = q_lat.shape
    q = (rmsnorm(q_lat, g_q) @ w_uq).reshape(B, S, MLA_HEADS, QK_NOPE + QK_ROPE)
    kv = (rmsnorm(kv_lat, g_kv) @ w_ukv).reshape(B, S, MLA_HEADS, QK_NOPE + V_DIM)
    q_nope, q_pe = q[..., :QK_NOPE], q[..., QK_NOPE:]
    k_nope, v = kv[..., :QK_NOPE], kv[..., QK_NOPE:]
    cos, sin = rope_tables(S)
    q_pe = apply_rope(q_pe, cos, sin)
    k_pe = apply_rope(k_rope[:, :, None, :], cos, sin)
    q = jnp.concatenate([q_nope, q_pe], axis=-1) * ((QK_NOPE + QK_ROPE) ** -0.5)
    k = jnp.concatenate([k_nope, jnp.broadcast_to(k_pe, (B, S, MLA_HEADS, QK_ROPE))], axis=-1)
    nb = S // ATTN_BLOCK
    qb = jnp.moveaxis(q.reshape(B, nb, ATTN_BLOCK, MLA_HEADS, QK_NOPE + QK_ROPE), 1, 0)

    def block(qi):
        s = jnp.einsum('bqhd,bkhd->bhqk', qi, k, preferred_element_type=jnp.float32)
        p = jax.nn.softmax(s, axis=-1)
        return jnp.einsum('bhqk,bkhd->bqhd', p.astype(v.dtype), v)

    o = lax.map(block, qb)
    return jnp.moveaxis(o, 0, 1).reshape(B, S, MLA_WIDTH)


def neighbourhood_attn(q, k, v, rpb):
    B, S, _ = q.shape
    rows = S // GRID_W
    kh = min(NA_WIN_H, rows)
    qg = q.reshape(B, rows, GRID_W, NA_HEADS, NA_DIM) * (NA_DIM ** -0.5)
    kg = k.reshape(B, rows, GRID_W, NA_HEADS, NA_DIM)
    vg = v.reshape(B, rows, GRID_W, NA_HEADS, NA_DIM)
    col = jnp.arange(GRID_W)
    col_start = jnp.clip(col - NA_WIN_W // 2, 0, GRID_W - NA_WIN_W)
    col_in = (col[None, :] >= col_start[:, None]) & (col[None, :] < col_start[:, None] + NA_WIN_W)
    dc_idx = jnp.clip(col[None, :] - col[:, None] + NA_WIN_W - 1, 0, 2 * NA_WIN_W - 2)

    def row_block(r):
        r0 = jnp.clip(r - NA_WIN_H // 2, 0, rows - kh)
        q_r = lax.dynamic_index_in_dim(qg, r, axis=1, keepdims=False)
        k_r = lax.dynamic_slice_in_dim(kg, r0, kh, axis=1)
        v_r = lax.dynamic_slice_in_dim(vg, r0, kh, axis=1)
        dr_idx = r0 + jnp.arange(kh) - r + NA_WIN_H - 1
        bias = rpb[:, dr_idx[:, None, None], dc_idx[None, :, :]]
        bias = bias.transpose(0, 2, 1, 3).astype(jnp.float32)
        s = jnp.einsum('bchd,bjkhd->bhcjk', q_r, k_r, preferred_element_type=jnp.float32)
        s = jnp.where(col_in[None, None, :, None, :], s + bias[None], -jnp.inf)
        p = jax.nn.softmax(s.reshape(B, NA_HEADS, GRID_W, kh * GRID_W), axis=-1)
        p = p.reshape(B, NA_HEADS, GRID_W, kh, GRID_W)
        return jnp.einsum('bhcjk,bjkhd->bchd', p.astype(v_r.dtype), v_r)

    o = lax.map(row_block, jnp.arange(rows))
    return jnp.moveaxis(o, 0, 1).reshape(B, S, NA_WIDTH)


def even_mixer(h, w_in0, g_q, g_kv, w_uq, w_ukv, rpb, w_out0):
    u = h @ w_in0
    q_lat, kv_lat, k_rope, na_q, na_k, na_v, gate = jnp.split(u, np.cumsum(IN0_SIZES)[:-1].tolist(), axis=-1)
    o_mla = mla(q_lat, kv_lat, k_rope, g_q, g_kv, w_uq, w_ukv)
    o_na = neighbourhood_attn(na_q, na_k, na_v, rpb)
    o = jnp.concatenate([o_mla, o_na], axis=-1) * jax.nn.silu(gate)
    return o @ w_out0


def centred_conv(x, w, b):
    y = lax.conv_general_dilated(x, w[:, None, :], window_strides=(1,),
                                 padding=[(ML_CONV // 2, ML_CONV // 2)],
                                 dimension_numbers=('NWC', 'WIO', 'NWC'),
                                 feature_group_count=x.shape[-1])
    return y + b


def headwise(x, w):
    B, S, _ = x.shape
    xb = x.reshape(B, S, QKV_NB, QKV_BLOCK)
    return jnp.einsum('bsni,noi->bsno', xb, w).reshape(B, S, ML_INNER)


def mlstm_scan(q, k, v, ig, lf):
    B, H, S, Dh = q.shape
    nc = S // ML_CHUNK
    L = ML_CHUNK

    def chunks(a):
        return jnp.moveaxis(a.reshape(B, H, nc, L, *a.shape[3:]), 2, 0)

    tri = jnp.tril(jnp.ones((L, L), dtype=bool))

    def step(carry, xs):
        C, n, m = carry
        qc, kc, vc, ic, fc = xs
        b = jnp.cumsum(fc, axis=-1)
        dlog = jnp.where(tri, b[..., :, None] - b[..., None, :] + ic[..., None, :], -jnp.inf)
        inter = b + m[..., None]
        m_t = jnp.maximum(inter, jnp.max(dlog, axis=-1))
        dmat = jnp.exp(dlog - m_t[..., None])
        w_inter = jnp.exp(inter - m_t)
        s = jnp.einsum('bhld,bhsd->bhls', qc, kc) * dmat
        num = w_inter[..., None] * jnp.einsum('bhvk,bhlk->bhlv', C, qc) + jnp.einsum('bhls,bhsv->bhlv', s, vc)
        den = w_inter * jnp.einsum('bhk,bhlk->bhl', n, qc) + jnp.sum(s, axis=-1)
        h = num / jnp.maximum(jnp.abs(den), jnp.exp(-m_t))[..., None]
        b_last = b[..., -1]
        g = b_last[..., None] - b + ic
        m_new = jnp.maximum(b_last + m, jnp.max(g, axis=-1))
        w_old = jnp.exp(b_last + m - m_new)
        w_s = jnp.exp(g - m_new[..., None])
        C = w_old[..., None, None] * C + jnp.einsum('bhsv,bhsk->bhvk', w_s[..., None] * vc, kc)
        n = w_old[..., None] * n + jnp.einsum('bhs,bhsk->bhk', w_s, kc)
        return (C, n, m_new), h

    init = (jnp.zeros((B, H, Dh, Dh), jnp.float32), jnp.zeros((B, H, Dh), jnp.float32),
            jnp.zeros((B, H), jnp.float32))
    _, hs = lax.scan(step, init, (chunks(q), chunks(k), chunks(v), chunks(ig), chunks(lf)))
    return jnp.moveaxis(hs, 0, 2).reshape(B, H, S, Dh)


def odd_mixer(h, w_in1, conv_w, conv_b, w_q, w_k, w_v, w_gate, b_gate, g_mh, skip, w_out1):
    B, S, _ = h.shape
    xm, z = jnp.split(h @ w_in1, 2, axis=-1)
    xc = jax.nn.silu(centred_conv(xm, conv_w, conv_b))
    q = headwise(xc, w_q)
    k = headwise(xc, w_k)
    v = headwise(xm, w_v)
    pre = (jnp.concatenate([q, k, v], axis=-1) @ w_gate + b_gate).astype(jnp.float32)
    pre = pre.transpose(0, 2, 1)
    i_f, f_f, i_b, f_b = jnp.split(pre, 4, axis=1)

    def heads(a):
        return a.reshape(B, S, ML_HEADS, ML_DIM).transpose(0, 2, 1, 3).astype(jnp.float32)

    qh = heads(q) * (ML_DIM ** -0.5)
    kh = heads(k)
    vh = heads(v)
    flip = lambda a: jnp.flip(a, axis=2)
    h_fwd = mlstm_scan(qh, kh, vh, i_f, jax.nn.log_sigmoid(f_f))
    h_bwd = flip(mlstm_scan(flip(qh), flip(kh), flip(vh), flip(i_b), flip(jax.nn.log_sigmoid(f_b))))
    hs = h_fwd + h_bwd
    mu = jnp.mean(hs, axis=-1, keepdims=True)
    var = jnp.mean(jnp.square(hs - mu), axis=-1, keepdims=True)
    hn = (hs - mu) * lax.rsqrt(var + EPS)
    hn = hn.transpose(0, 2, 1, 3).reshape(B, S, ML_INNER) * g_mh.astype(jnp.float32)
    out = (hn.astype(xm.dtype) + skip * xc) * jax.nn.silu(z)
    return out @ w_out1


def trunk(x, c, g_norm, w_ada, b_ada, g_final, even_p, odd_p):
    for l in range(DEPTH):
        mod = jax.nn.silu(c) @ w_ada[l] + b_ada[l]
        shift, scale, gate = jnp.split(mod, 3, axis=-1)
        h = rmsnorm(x, g_norm[l]) * (1 + scale[:, None, :]) + shift[:, None, :]
        j = l // 2
        if l % 2 == 0:
            out = even_mixer(h, *[p[j] for p in even_p])
        else:
            out = odd_mixer(h, *[p[j] for p in odd_p])
        x = x + gate[:, None, :] * out
    return rmsnorm(x, g_final)


def setup_inputs(seed: int = 0) -> dict:
    key = jax.random.key(seed)
    ks = jax.random.split(key, 32)
    nrm = lambda k, shape, fan_in: jax.random.normal(k, shape, jnp.float32) * (fan_in ** -0.5)
    D = D_MODEL
    H = ML_HEADS
    b_gate = jnp.concatenate([
        0.1 * jax.random.normal(ks[20], (N_ODD, H), jnp.float32),
        3.0 + 3.0 * jax.random.uniform(ks[21], (N_ODD, H), jnp.float32),
        0.1 * jax.random.normal(ks[22], (N_ODD, H), jnp.float32),
        3.0 + 3.0 * jax.random.uniform(ks[23], (N_ODD, H), jnp.float32)], axis=-1)
    return {
        'x_prompt': jax.random.normal(ks[0], (BATCH, SEQ, D), jnp.float32),
        'x_sample': jax.random.normal(ks[1], (DEC_BATCH, DEC_SEQ, D), jnp.float32),
        'c_prompt': jax.random.normal(ks[2], (BATCH, D), jnp.float32),
        'c_sample': jax.random.normal(ks[3], (DEC_BATCH, D), jnp.float32),
        'g_norm': 1.0 + 0.02 * jax.random.normal(ks[4], (DEPTH, D), jnp.float32),
        'w_ada': nrm(ks[5], (DEPTH, D, 3 * D), D) * 0.5,
        'b_ada': 0.02 * jax.random.normal(ks[6], (DEPTH, 3 * D), jnp.float32),
        'g_final': 1.0 + 0.02 * jax.random.normal(ks[7], (D,), jnp.float32),
        'w_in0': nrm(ks[8], (N_EVEN, D, IN0_WIDTH), D),
        'g_qlat': 1.0 + 0.02 * jax.random.normal(ks[9], (N_EVEN, Q_LORA), jnp.float32),
        'g_kvlat': 1.0 + 0.02 * jax.random.normal(ks[10], (N_EVEN, KV_LORA), jnp.float32),
        'w_uq': nrm(ks[11], (N_EVEN, Q_LORA, MLA_HEADS * (QK_NOPE + QK_ROPE)), Q_LORA),
        'w_ukv': nrm(ks[12], (N_EVEN, KV_LORA, MLA_HEADS * (QK_NOPE + V_DIM)), KV_LORA),
        'na_rpb': 0.1 * jax.random.normal(ks[13], (N_EVEN, NA_HEADS, 2 * NA_WIN_H - 1, 2 * NA_WIN_W - 1), jnp.float32),
        'w_out0': nrm(ks[14], (N_EVEN, MIX0_WIDTH, D), MIX0_WIDTH),
        'w_in1': nrm(ks[15], (N_ODD, D, 2 * ML_INNER), D),
        'conv_w': nrm(ks[16], (N_ODD, ML_CONV, ML_INNER), ML_CONV),
        'conv_b': 0.02 * jax.random.normal(ks[17], (N_ODD, ML_INNER), jnp.float32),
        'w_q': nrm(ks[18], (N_ODD, QKV_NB, QKV_BLOCK, QKV_BLOCK), QKV_BLOCK),
        'w_k': nrm(ks[19], (N_ODD, QKV_NB, QKV_BLOCK, QKV_BLOCK), QKV_BLOCK),
        'w_v': nrm(ks[24], (N_ODD, QKV_NB, QKV_BLOCK, QKV_BLOCK), QKV_BLOCK),
        'w_gate': nrm(ks[25], (N_ODD, 3 * ML_INNER, 4 * H), 3 * ML_INNER),
        'b_gate': b_gate,
        'g_mh': 1.0 + 0.02 * jax.random.normal(ks[26], (N_ODD, ML_INNER), jnp.float32),
        'skip': 1.0 + 0.02 * jax.random.normal(ks[27], (N_ODD, ML_INNER), jnp.float32),
        'w_out1': nrm(ks[28], (N_ODD, ML_INNER, D), ML_INNER),
    }


def reference(x_prompt, x_sample, c_prompt, c_sample, g_norm, w_ada, b_ada, g_final,
              w_in0, g_qlat, g_kvlat, w_uq, w_ukv, na_rpb, w_out0,
              w_in1, conv_w, conv_b, w_q, w_k, w_v, w_gate, b_gate, g_mh, skip, w_out1):
    even_p = (w_in0, g_qlat, g_kvlat, w_uq, w_ukv, na_rpb, w_out0)
    odd_p = (w_in1, conv_w, conv_b, w_q, w_k, w_v, w_gate, b_gate, g_mh, skip, w_out1)
    y_prompt = trunk(x_prompt, c_prompt, g_norm, w_ada, b_ada, g_final, even_p, odd_p)
    y_sample = trunk(x_sample, c_sample, g_norm, w_ada, b_ada, g_final, even_p, odd_p)
    return (y_prompt, y_sample)
```

```python
import functools

import numpy as np
import jax
import jax.numpy as jnp
from jax import lax
from jax.experimental import pallas as pl
from jax.experimental.pallas import tpu as pltpu

F32 = jnp.float32
BF16 = jnp.bfloat16

D_MODEL = 1024
DEPTH = 2
GRID_W = 64
EPS = 1e-6
MLA_HEADS = 8
QK_NOPE = 64
QK_ROPE = 32
V_DIM = 64
Q_LORA = 256
KV_LORA = 256
ROPE_THETA = 10000.0
MLA_WIDTH = MLA_HEADS * V_DIM
NA_HEADS = 8
NA_DIM = 64
NA_WIN_H = 8
NA_WIN_W = 16
NA_WIDTH = NA_HEADS * NA_DIM
ML_HEADS = 4
ML_INNER = 2 * D_MODEL
ML_DIM = ML_INNER // ML_HEADS
ML_CONV = 5
QKV_BLOCK = 4
ML_CHUNK = 128

HEAD_PAD = 128
MLA_QK = MLA_HEADS * HEAD_PAD
ROPE_HALF = QK_ROPE // 2
LANES = 128
BD_TILE = 256
N_BD = ML_INNER // BD_TILE
NEG_BIG = -1e30

VMEM_LIMIT = 56 * 1024 * 1024

_NT = (((1,), (1,)), ((), ()))
_TN = (((0,), (0,)), ((), ()))


def _cparams(sem):
    return pltpu.CompilerParams(dimension_semantics=sem, vmem_limit_bytes=VMEM_LIMIT)


def _silu(x):
    return x * jax.nn.sigmoid(x)


def _rms(x, g):
    return x * lax.rsqrt(jnp.mean(x * x, axis=-1, keepdims=True) + EPS) * g


def _ada_kernel(c_ref, w_ref, b_ref, o_ref):
    c = c_ref[...]
    o_ref[0] = jnp.dot(_silu(c).astype(BF16), w_ref[0].astype(BF16), preferred_element_type=F32) + b_ref[0]


def _ada(c_all, w_ada, b_ada):
    nb = c_all.shape[0]
    tn = 512
    return pl.pallas_call(
        _ada_kernel,
        grid=(DEPTH, 3 * D_MODEL // tn),
        in_specs=[pl.BlockSpec((nb, D_MODEL), lambda l, j: (0, 0)),
                  pl.BlockSpec((1, D_MODEL, tn), lambda l, j: (l, 0, j)),
                  pl.BlockSpec((1, 1, tn), lambda l, j: (l, 0, j))],
        out_specs=pl.BlockSpec((1, nb, tn), lambda l, j: (l, 0, j)),
        out_shape=jax.ShapeDtypeStruct((DEPTH, nb, 3 * D_MODEL), F32),
        compiler_params=_cparams(("arbitrary", "arbitrary")),
        name="ada",
    )(c_all, w_ada, b_ada.reshape(DEPTH, 1, 3 * D_MODEL))


def _in0_kernel(x_ref, mod_ref, g_ref, wa_ref, wb_ref, gq_ref, gkv_ref, wuq_ref, wukv_ref,
                rc_ref, rm_ref, rp_ref,
                q_ref, k_ref, v_ref, nq_ref, nk_ref, nv_ref, gate_ref):
    x = x_ref[0]
    shift = mod_ref[0, 0:1, :]
    scale = mod_ref[0, 1:2, :]
    h = _rms(x, g_ref[...]) * (1.0 + scale) + shift
    hb = h.astype(BF16)

    ua = jnp.dot(hb, wa_ref[...], preferred_element_type=F32)
    qn = _rms(ua[:, :Q_LORA], gq_ref[...]).astype(BF16)
    kvn = _rms(ua[:, Q_LORA:Q_LORA + KV_LORA], gkv_ref[...]).astype(BF16)
    kr = ua[:, Q_LORA + KV_LORA:]
    qf = jnp.dot(qn, wuq_ref[...], preferred_element_type=F32)
    kvf = jnp.dot(kvn, wukv_ref[...], preferred_element_type=F32)

    rc, rm, rp = rc_ref[...], rm_ref[...], rp_ref[...]

    def rope(t):
        return (t * rc + pltpu.roll(t, LANES - ROPE_HALF, 1) * rm + pltpu.roll(t, ROPE_HALF, 1) * rp)

    krr = rope(kr)
    qscale = (QK_NOPE + QK_ROPE) ** -0.5
    for hh in range(MLA_HEADS):
        sl = slice(hh * HEAD_PAD, (hh + 1) * HEAD_PAD)
        q_ref[0, :, sl] = (rope(qf[:, sl]) * qscale).astype(BF16)
        k_ref[0, :, sl] = (kvf[:, sl] + krr).astype(BF16)
    v_ref[0] = kvf[:, MLA_QK:].astype(BF16)

    ub = jnp.dot(hb, wb_ref[...], preferred_element_type=F32)
    nq_ref[0] = (ub[:, :NA_WIDTH] * (NA_DIM ** -0.5)).astype(BF16)
    nk_ref[0] = ub[:, NA_WIDTH:2 * NA_WIDTH].astype(BF16)
    nv_ref[0] = ub[:, 2 * NA_WIDTH:3 * NA_WIDTH].astype(BF16)
    gate_ref[0] = ub[:, 3 * NA_WIDTH:].astype(BF16)


def _in0(x, mod, g, wa, wb, gq, gkv, wuq, wukv, rc, rm, rp, tm=512):
    B, S, D = x.shape
    full = lambda a: pl.BlockSpec(a.shape, lambda b, i: (0,) * a.ndim)
    tok = lambda w: pl.BlockSpec((1, tm, w), lambda b, i: (b, i, 0))
    tab = pl.BlockSpec((tm, LANES), lambda b, i: (i, 0))
    widths = (MLA_QK, MLA_QK, MLA_WIDTH, NA_WIDTH, NA_WIDTH, NA_WIDTH, MLA_WIDTH + NA_WIDTH)
    return pl.pallas_call(
        _in0_kernel,
        grid=(B, S // tm),
        in_specs=[tok(D), pl.BlockSpec((1, 3, D), lambda b, i: (b, 0, 0)), full(g), full(wa), full(wb),
                  full(gq), full(gkv), full(wuq), full(wukv), tab, tab, tab],
        out_specs=[tok(w) for w in widths],
        out_shape=[jax.ShapeDtypeStruct((B, S, w), BF16) for w in widths],
        compiler_params=_cparams(("parallel", "parallel")),
        name="in0",
    )(x, mod, g, wa, wb, gq, gkv, wuq, wukv, rc, rm, rp)


def _mla_kernel(q_ref, k_ref, v_ref, o_ref):
    v = v_ref[0]
    outs = []
    for j in range(2):
        sl = slice(j * HEAD_PAD, (j + 1) * HEAD_PAD)
        s = lax.dot_general(q_ref[0, :, sl], k_ref[0, :, sl], _NT, preferred_element_type=F32)
        m = jnp.max(s, axis=-1, keepdims=True)
        p = jnp.exp(s - m)
        l = jnp.sum(p, axis=-1, keepdims=True)
        outs.append(jnp.dot(p.astype(BF16), v, preferred_element_type=F32) / l)
    lane = lax.broadcasted_iota(jnp.int32, outs[0].shape, 1)
    o_ref[0] = jnp.where(lane < V_DIM, outs[0], outs[1]).astype(BF16)


def _mla(q, k, v, tq=512):
    B, S, _ = q.shape
    return pl.pallas_call(
        _mla_kernel,
        grid=(B, MLA_HEADS // 2, S // tq),
        in_specs=[pl.BlockSpec((1, tq, 2 * HEAD_PAD), lambda b, h, i: (b, i, h)),
                  pl.BlockSpec((1, S, 2 * HEAD_PAD), lambda b, h, i: (b, 0, h)),
                  pl.BlockSpec((1, S, 2 * V_DIM), lambda b, h, i: (b, 0, h))],
        out_specs=pl.BlockSpec((1, tq, 2 * V_DIM), lambda b, h, i: (b, i, h)),
        out_shape=jax.ShapeDtypeStruct((B, S, MLA_WIDTH), BF16),
        compiler_params=_cparams(("parallel", "parallel", "arbitrary")),
        name="mla",
    )(q, k, v)


NA_GROUP = 4
NA_LANES = NA_GROUP * NA_DIM
NA_KEYS = NA_WIN_H * GRID_W


def _na_kernel(q_ref, k_ref, v_ref, bias_ref, o_ref, *, rows):
    rh = lax.broadcasted_iota(jnp.int32, (NA_LANES, NA_LANES), 0) // GRID_W
    lh = lax.broadcasted_iota(jnp.int32, (NA_LANES, NA_LANES), 1) // NA_DIM
    hmask = (rh == lh).astype(F32)

    def body(r, carry):
        r0 = jnp.clip(r - NA_WIN_H // 2, 0, rows - NA_WIN_H)
        d = r - r0
        qr = q_ref[0, pl.ds(pl.multiple_of(r * GRID_W, GRID_W), GRID_W), :].astype(F32)
        qs = (jnp.concatenate([qr] * NA_GROUP, axis=0) * hmask).astype(BF16)
        win = pl.ds(pl.multiple_of(r0 * GRID_W, GRID_W), NA_KEYS)
        s = lax.dot_general(qs, k_ref[0, win, :], _NT, preferred_element_type=F32) + bias_ref[0, d]
        m = jnp.max(s, axis=-1, keepdims=True)
        p = jnp.exp(s - m)
        l = jnp.sum(p, axis=-1, keepdims=True)
        o = jnp.dot(p.astype(BF16), v_ref[0, win, :], preferred_element_type=F32) / l * hmask
        out = o[0:GRID_W]
        for g in range(1, NA_GROUP):
            out = out + o[g * GRID_W:(g + 1) * GRID_W]
        o_ref[0, pl.ds(pl.multiple_of(r * GRID_W, GRID_W), GRID_W), :] = out.astype(BF16)
        return carry

    lax.fori_loop(0, rows, body, 0)


def _natten(q, k, v, bias):
    B, S, _ = q.shape
    rows = S // GRID_W
    tok = pl.BlockSpec((1, S, NA_LANES), lambda g, b: (b, 0, g))
    return pl.pallas_call(
        functools.partial(_na_kernel, rows=rows),
        grid=(NA_HEADS // NA_GROUP, B),
        in_specs=[tok, tok, tok,
                  pl.BlockSpec((1, NA_WIN_H, NA_LANES, NA_KEYS), lambda g, b: (g, 0, 0, 0))],
        out_specs=tok,
        out_shape=jax.ShapeDtypeStruct((B, S, NA_WIDTH), BF16),
        compiler_params=_cparams(("parallel", "parallel")),
        name="natten",
    )(q, k, v, bias)


def _mid_kernel(om_ref, on_ref, gate_ref, x_ref, mod0_ref, mod1_ref, g_ref, wo_ref, wi_ref,
                x1_ref, xm_ref, z_ref):
    sg = _silu(gate_ref[0].astype(F32))
    o = jnp.concatenate([om_ref[0].astype(F32), on_ref[0].astype(F32)], axis=-1) * sg
    out = jnp.dot(o.astype(BF16), wo_ref[...], preferred_element_type=F32)
    x1 = x_ref[0] + mod0_ref[0, 2:3, :] * out
    x1_ref[0] = x1
    h = _rms(x1, g_ref[...]) * (1.0 + mod1_ref[0, 1:2, :]) + mod1_ref[0, 0:1, :]
    u = jnp.dot(h.astype(BF16), wi_ref[...], preferred_element_type=F32)
    xm_ref[0] = u[:, :ML_INNER].astype(BF16)
    z_ref[0] = u[:, ML_INNER:].astype(BF16)


def _mid(om, on, gate, x, mod0, mod1, g, wo, wi, tm=512):
    B, S, D = x.shape
    full = lambda a: pl.BlockSpec(a.shape, lambda b, i: (0,) * a.ndim)
    tok = lambda w: pl.BlockSpec((1, tm, w), lambda b, i: (b, i, 0))
    modspec = pl.BlockSpec((1, 3, D), lambda b, i: (b, 0, 0))
    return pl.pallas_call(
        _mid_kernel,
        grid=(B, S // tm),
        in_specs=[tok(MLA_WIDTH), tok(NA_WIDTH), tok(D), tok(D), modspec, modspec, full(g), full(wo), full(wi)],
        out_specs=[tok(D), tok(ML_INNER), tok(ML_INNER)],
        out_shape=[jax.ShapeDtypeStruct((B, S, D), F32),
                   jax.ShapeDtypeStruct((B, S, ML_INNER), BF16),
                   jax.ShapeDtypeStruct((B, S, ML_INNER), BF16)],
        compiler_params=_cparams(("parallel", "parallel")),
        name="mid",
    )(om, on, gate, x, mod0, mod1, g, wo, wi)


HALO = 16
PADR = 8


def _log_sigmoid(x):
    return jnp.minimum(x, 0.0) - jnp.log1p(jnp.exp(-jnp.abs(x)))


def _split3(x):
    hi = x.astype(BF16)
    r1 = x - hi.astype(F32)
    mid = r1.astype(BF16)
    lo = (r1 - mid.astype(F32)).astype(BF16)
    return hi, mid, lo


def _conv_kernel(xm_ref, xp_ref, xn_ref, cw_ref, cb_ref, bdqk_ref, bdv_ref, wg_ref, bg_ref, t1_ref, t2_ref,
                 q_ref, k_ref, v_ref, xc_ref, gc_ref, gr_ref, ext_ref, *, tm):
    i = pl.program_id(1)
    n = pl.num_programs(1)
    prev = xp_ref[0].astype(F32)[HALO - PADR:, :]
    nxt = xn_ref[0].astype(F32)[:PADR, :]
    ext_ref[0:PADR, :] = jnp.where(i > 0, prev, 0.0)
    ext_ref[PADR:PADR + tm, :] = xm_ref[0].astype(F32)
    ext_ref[PADR + tm:, :] = jnp.where(i < n - 1, nxt, 0.0)

    qscale = ML_DIM ** -0.5
    pre = jnp.zeros((tm, LANES), F32) + bg_ref[...]
    for c in range(N_BD):
        cs = slice(c * BD_TILE, (c + 1) * BD_TILE)
        y = jnp.zeros((tm, BD_TILE), F32) + cb_ref[:, cs]
        for j in range(ML_CONV):
            y = y + ext_ref[PADR - ML_CONV // 2 + j:PADR - ML_CONV // 2 + j + tm, cs] * cw_ref[j:j + 1, cs]
        xc = _silu(y)
        xcb = xc.astype(BF16)
        xc_ref[0, :, cs] = xcb
        qk = jnp.dot(xcb, bdqk_ref[c], preferred_element_type=F32)
        vv = jnp.dot(xm_ref[0, :, cs], bdv_ref[c], preferred_element_type=F32)
        qc, kc = qk[:, :BD_TILE], qk[:, BD_TILE:]
        qkv = jnp.concatenate([qc, kc, vv], axis=-1).astype(BF16)
        pre = pre + jnp.dot(qkv, wg_ref[c], preferred_element_type=F32)
        q_ref[0, :, cs] = (qc * qscale).astype(BF16)
        k_ref[0, :, cs] = kc.astype(BF16)
        v_ref[0, :, cs] = vv.astype(BF16)

    lf = _log_sigmoid(pre)
    cum_f = jnp.zeros((tm, LANES), F32)
    cum_b = jnp.zeros((tm, LANES), F32)
    for part in _split3(lf):
        cum_f = cum_f + jnp.dot(t1_ref[...], part, preferred_element_type=F32)
        cum_b = cum_b + jnp.dot(t2_ref[...], part, preferred_element_type=F32)
    qty = lax.broadcasted_iota(jnp.int32, (tm, LANES), 1) % 4
    g = jnp.where(qty == 1, cum_f, jnp.where(qty == 3, cum_b, pre))
    gc_ref[0] = g
    gr_ref[0] = g.T[:4 * ML_HEADS, :]


def _conv(xm, cw, cb, bdqk, bdv, wg, bg, t1, t2, tm=512):
    B, S, _ = xm.shape
    hb = tm // HALO
    full = lambda a: pl.BlockSpec(a.shape, lambda b, i: (0,) * a.ndim)
    tok = lambda w: pl.BlockSpec((1, tm, w), lambda b, i: (b, i, 0))
    return pl.pallas_call(
        functools.partial(_conv_kernel, tm=tm),
        grid=(B, S // tm),
        in_specs=[tok(ML_INNER),
                  pl.BlockSpec((1, HALO, ML_INNER), lambda b, i: (b, jnp.maximum(i * hb - 1, 0), 0)),
                  pl.BlockSpec((1, HALO, ML_INNER), lambda b, i: (b, jnp.minimum((i + 1) * hb, S // HALO - 1), 0)),
                  full(cw), full(cb), full(bdqk), full(bdv), full(wg), full(bg), full(t1), full(t2)],
        out_specs=[tok(ML_INNER), tok(ML_INNER), tok(ML_INNER), tok(ML_INNER), tok(LANES),
                   pl.BlockSpec((1, 4 * ML_HEADS, tm), lambda b, i: (b, 0, i))],
        out_shape=[jax.ShapeDtypeStruct((B, S, ML_INNER), BF16)] * 4
        + [jax.ShapeDtypeStruct((B, S, LANES), F32), jax.ShapeDtypeStruct((B, 4 * ML_HEADS, S), F32)],
        scratch_shapes=[pltpu.VMEM((tm + 2 * PADR, ML_INNER), F32)],
        compiler_params=_cparams(("parallel", "parallel")),
        name="conv",
    )(xm, xm, xm, cw, cb, bdqk, bdv, wg, bg, t1, t2)


def _mlstm_kernel(q_ref, k_ref, v_ref, gc_ref, gr_ref, gmh_ref, o_ref,
                  ctf_ref, ctb_ref, hf_ref, hb_ref, *, seq):
    L = ML_CHUNK
    nc = seq // L
    head = pl.program_id(1)
    ctf_ref[...] = jnp.zeros_like(ctf_ref)
    ctb_ref[...] = jnp.zeros_like(ctb_ref)
    lane = lax.broadcasted_iota(jnp.int32, (L, LANES), 1)
    ri = lax.broadcasted_iota(jnp.int32, (L, L), 0)
    ci = lax.broadcasted_iota(jnp.int32, (L, L), 1)

    def chunk(c, ct_ref, nvec, m, qi, qb, lower):
        rows = pl.ds(pl.multiple_of(c * L, L), L)
        qc, kc, vc = q_ref[0, rows, :], k_ref[0, rows, :], v_ref[0, rows, :]
        gcol = gc_ref[0, rows, :]

        def col(j):
            return jnp.sum(jnp.where(lane == head * 4 + j, gcol, 0.0), axis=1, keepdims=True)

        i_col, b_col = col(qi), col(qb)
        i_row = gr_ref[0, 0, qi:qi + 1, rows]
        b_row = gr_ref[0, 0, qb:qb + 1, rows]
        mask = (ci <= ri) if lower else (ci >= ri)
        dlog = jnp.where(mask, b_col - b_row + i_row, -jnp.inf)
        inter = b_col + m
        m_t = jnp.maximum(inter, jnp.max(dlog, axis=-1, keepdims=True))
        dmat = jnp.exp(dlog - m_t)
        w_inter = jnp.exp(inter - m_t)
        s = lax.dot_general(qc, kc, _NT, preferred_element_type=F32) * dmat
        num = (w_inter * jnp.dot(qc, ct_ref[...].astype(BF16), preferred_element_type=F32)
               + jnp.dot(s.astype(BF16), vc, preferred_element_type=F32))
        qn = jnp.sum(qc.astype(F32) * nvec, axis=-1, keepdims=True)
        den = w_inter * qn + jnp.sum(s, axis=-1, keepdims=True)
        hh = num / jnp.maximum(jnp.abs(den), jnp.exp(-m_t))
        b_last = b_col[L - 1:L, :] if lower else b_col[0:1, :]
        g_col = b_last - b_col + i_col
        m_new = jnp.maximum(b_last + m, jnp.max(g_col, axis=0, keepdims=True))
        w_old = jnp.exp(b_last + m - m_new)
        w_s = jnp.exp(g_col - m_new)
        wv = (w_s * vc.astype(F32)).astype(BF16)
        ct_ref[...] = w_old * ct_ref[...] + lax.dot_general(kc, wv, _TN, preferred_element_type=F32)
        nvec = w_old * nvec + jnp.sum(w_s * kc.astype(F32), axis=0, keepdims=True)
        return rows, hh, nvec, m_new

    def body(c, carry):
        nf, mf, nb, mb = carry
        rows_f, hf, nf, mf = chunk(c, ctf_ref, nf, mf, 0, 1, True)
        rows_b, hb, nb, mb = chunk(nc - 1 - c, ctb_ref, nb, mb, 2, 3, False)
        hf_ref[rows_f, :] = hf
        hb_ref[rows_b, :] = hb
        return nf, mf, nb, mb

    z_n = jnp.zeros((1, ML_DIM), F32)
    z_m = jnp.zeros((1, 1), F32)
    lax.fori_loop(0, nc, body, (z_n, z_m, z_n, z_m))

    te = 256
    for t in range(seq // te):
        rs = slice(t * te, (t + 1) * te)
        hs = hf_ref[rs, :] + hb_ref[rs, :]
        mu = jnp.mean(hs, axis=-1, keepdims=True)
        var = jnp.mean(jnp.square(hs - mu), axis=-1, keepdims=True)
        o_ref[0, rs, :] = ((hs - mu) * lax.rsqrt(var + EPS) * gmh_ref[...]).astype(BF16)


def _mlstm(q, k, v, gc, gr, gmh):
    B, S, _ = q.shape
    tok = pl.BlockSpec((1, S, ML_DIM), lambda b, h: (b, 0, h))
    return pl.pallas_call(
        functools.partial(_mlstm_kernel, seq=S),
        grid=(B, ML_HEADS),
        in_specs=[tok, tok, tok,
                  pl.BlockSpec((1, S, LANES), lambda b, h: (b, 0, 0)),
                  pl.BlockSpec((1, 1, 4, S), lambda b, h: (b, h, 0, 0)),
                  pl.BlockSpec((1, ML_DIM), lambda b, h: (0, h))],
        out_specs=tok,
        out_shape=jax.ShapeDtypeStruct((B, S, ML_INNER), BF16),
        scratch_shapes=[pltpu.VMEM((ML_DIM, ML_DIM), F32), pltpu.VMEM((ML_DIM, ML_DIM), F32),
                        pltpu.VMEM((S, ML_DIM), F32), pltpu.VMEM((S, ML_DIM), F32)],
        compiler_params=_cparams(("parallel", "arbitrary")),
        name="mlstm",
    )(q, k, v, gc, gr, gmh)


def _out_kernel(hn_ref, xc_ref, z_ref, x1_ref, mod1_ref, skip_ref, wo_ref, gf_ref, y_ref):
    o = (hn_ref[0].astype(F32) + skip_ref[...] * xc_ref[0].astype(F32)) * _silu(z_ref[0].astype(F32))
    out = jnp.dot(o.astype(BF16), wo_ref[...], preferred_element_type=F32)
    x2 = x1_ref[0] + mod1_ref[0, 2:3, :] * out
    y_ref[0] = _rms(x2, gf_ref[...])


def _out(hn, xc, z, x1, mod1, skip, wo, gf, tm=512):
    B, S, D = x1.shape
    full = lambda a: pl.BlockSpec(a.shape, lambda b, i: (0,) * a.ndim)
    tok = lambda w: pl.BlockSpec((1, tm, w), lambda b, i: (b, i, 0))
    return pl.pallas_call(
        _out_kernel,
        grid=(B, S // tm),
        in_specs=[tok(ML_INNER), tok(ML_INNER), tok(ML_INNER), tok(D),
                  pl.BlockSpec((1, 3, D), lambda b, i: (b, 0, 0)), full(skip), full(wo), full(gf)],
        out_specs=tok(D),
        out_shape=jax.ShapeDtypeStruct((B, S, D), F32),
        compiler_params=_cparams(("parallel", "parallel")),
        name="out",
    )(hn, xc, z, x1, mod1, skip, wo, gf)


def _rope_tables(S):
    pos = jnp.arange(S, dtype=F32)
    inv = 1.0 / (ROPE_THETA ** (jnp.arange(0, QK_ROPE, 2, dtype=F32) / QK_ROPE))
    ang = pos[:, None] * inv[None, :]
    cos, sin = jnp.cos(ang), jnp.sin(ang)
    ones = jnp.ones((S, QK_NOPE), F32)
    z16 = jnp.zeros((S, ROPE_HALF), F32)
    z32 = jnp.zeros((S, HEAD_PAD - QK_NOPE - QK_ROPE), F32)
    z64 = jnp.zeros((S, QK_NOPE), F32)
    rc = jnp.concatenate([ones, cos, cos, z32], axis=-1)
    rm = jnp.concatenate([z64, -sin, z16, z32], axis=-1)
    rp = jnp.concatenate([z64, z16, sin, z32], axis=-1)
    return rc, rm, rp


def _na_bias(rpb):
    col = np.arange(GRID_W)
    col_start = np.clip(col - NA_WIN_W // 2, 0, GRID_W - NA_WIN_W)
    col_in = (col[None, :] >= col_start[:, None]) & (col[None, :] < col_start[:, None] + NA_WIN_W)
    dc_idx = np.clip(col[None, :] - col[:, None] + NA_WIN_W - 1, 0, 2 * NA_WIN_W - 2)
    d = np.arange(NA_WIN_H)
    dr_idx = np.arange(NA_WIN_H)[None, :] - d[:, None] + NA_WIN_H - 1
    b = rpb[:, dr_idx[:, :, None, None], dc_idx[None, None, :, :]]
    b = jnp.where(col_in[None, None, None], b, NEG_BIG)
    b = b.transpose(0, 1, 3, 2, 4).reshape(NA_HEADS, NA_WIN_H, GRID_W, NA_KEYS)
    b = b.reshape(NA_HEADS // NA_GROUP, NA_GROUP, NA_WIN_H, GRID_W, NA_KEYS).transpose(0, 2, 1, 3, 4)
    return b.reshape(NA_HEADS // NA_GROUP, NA_WIN_H, NA_LANES, NA_KEYS).astype(F32)


def _block_diag(w):
    per = BD_TILE // QKV_BLOCK
    wt = w.transpose(0, 2, 1).reshape(N_BD, per, QKV_BLOCK, QKV_BLOCK)
    eye = jnp.eye(per, dtype=w.dtype)
    return jnp.einsum('caio,ab->caibo', wt, eye).reshape(N_BD, BD_TILE, BD_TILE)


def _tri(tm, lower):
    t = np.arange(tm)
    same = (t[:, None] // ML_CHUNK) == (t[None, :] // ML_CHUNK)
    tri = (t[None, :] <= t[:, None]) if lower else (t[None, :] >= t[:, None])
    return jnp.asarray(same & tri, dtype=BF16)


def _prep(g_norm, g_final, w_in0, g_qlat, g_kvlat, w_uq, w_ukv, na_rpb, w_out0,
          w_in1, conv_w, conv_b, w_q, w_k, w_v, w_gate, b_gate, g_mh, skip, w_out1):
    p = {}
    w0 = w_in0[0]
    o = np.cumsum((Q_LORA, KV_LORA, QK_ROPE, NA_WIDTH, NA_WIDTH, NA_WIDTH))
    kr = jnp.zeros((D_MODEL, HEAD_PAD), F32).at[:, QK_NOPE:QK_NOPE + QK_ROPE].set(w0[:, o[1]:o[2]])
    p['wa'] = jnp.concatenate([w0[:, :o[1]], kr], axis=-1).astype(BF16)
    p['wb'] = w0[:, o[2]:].astype(BF16)
    wq = w_uq[0].reshape(Q_LORA, MLA_HEADS, QK_NOPE + QK_ROPE)
    p['wuq'] = jnp.pad(wq, ((0, 0), (0, 0), (0, HEAD_PAD - QK_NOPE - QK_ROPE))).reshape(Q_LORA, MLA_QK).astype(BF16)
    wkv = w_ukv[0].reshape(KV_LORA, MLA_HEADS, QK_NOPE + V_DIM)
    wk = jnp.pad(wkv[:, :, :QK_NOPE], ((0, 0), (0, 0), (0, HEAD_PAD - QK_NOPE))).reshape(KV_LORA, MLA_QK)
    wv = wkv[:, :, QK_NOPE:].reshape(KV_LORA, MLA_WIDTH)
    p['wukv'] = jnp.concatenate([wk, wv], axis=-1).astype(BF16)
    p['gq'] = g_qlat[0].reshape(1, Q_LORA)
    p['gkv'] = g_kvlat[0].reshape(1, KV_LORA)
    p['g0'] = g_norm[0].reshape(1, D_MODEL)
    p['g1'] = g_norm[1].reshape(1, D_MODEL)
    p['gf'] = g_final.reshape(1, D_MODEL)
    p['bias'] = _na_bias(na_rpb[0])
    p['wo0'] = w_out0[0].astype(BF16)
    p['wi1'] = w_in1[0].astype(BF16)
    p['cw'] = conv_w[0]
    p['cb'] = conv_b[0].reshape(1, ML_INNER)
    p['bdqk'] = jnp.concatenate([_block_diag(w_q[0]), _block_diag(w_k[0])], axis=-1).astype(BF16)
    p['bdv'] = _block_diag(w_v[0]).astype(BF16)
    wg = w_gate[0].reshape(3, N_BD, BD_TILE, 4, ML_HEADS).transpose(1, 0, 2, 4, 3)
    wg = wg.reshape(N_BD, 3 * BD_TILE, 4 * ML_HEADS)
    p['wg'] = jnp.pad(wg, ((0, 0), (0, 0), (0, LANES - 4 * ML_HEADS))).astype(BF16)
    bg = b_gate[0].reshape(4, ML_HEADS).T.reshape(1, 4 * ML_HEADS)
    p['bg'] = jnp.pad(bg, ((0, 0), (0, LANES - 4 * ML_HEADS)))
    p['gmh'] = g_mh[0].reshape(1, ML_INNER)
    p['skip'] = skip[0].reshape(1, ML_INNER)
    p['wo1'] = w_out1[0].astype(BF16)
    return p


def _trunk(x, mod0, mod1, p, tabs, tm=512):
    B, S, _ = x.shape
    rc, rm, rp = tabs
    q, k, v, nq, nk, nv, gate = _in0(x, mod0, p['g0'], p['wa'], p['wb'], p['gq'], p['gkv'],
                                      p['wuq'], p['wukv'], rc, rm, rp, tm=tm)
    o_mla = _mla(q, k, v)
    o_na = _natten(nq, nk, nv, p['bias'])
    x1, xm, z = _mid(o_mla, o_na, gate, x, mod0, mod1, p['g1'], p['wo0'], p['wi1'], tm=tm)
    t1, t2 = _tri(tm, True), _tri(tm, False)
    mq, mk, mv, xc, gc, gr = _conv(xm, p['cw'], p['cb'], p['bdqk'], p['bdv'], p['wg'], p['bg'], t1, t2, tm=tm)
    hn = _mlstm(mq, mk, mv, gc, gr.reshape(B, ML_HEADS, 4, S), p['gmh'])
    return _out(hn, xc, z, x1, mod1, p['skip'], p['wo1'], p['gf'], tm=tm)


def kernel(x_prompt, x_sample, c_prompt, c_sample, g_norm, w_ada, b_ada, g_final, w_in0, g_qlat, g_kvlat, w_uq, w_ukv, na_rpb, w_out0, w_in1, conv_w, conv_b, w_q, w_k, w_v, w_gate, b_gate, g_mh, skip, w_out1):
    p = _prep(g_norm, g_final, w_in0, g_qlat, g_kvlat, w_uq, w_ukv, na_rpb, w_out0,
              w_in1, conv_w, conv_b, w_q, w_k, w_v, w_gate, b_gate, g_mh, skip, w_out1)
    nb_p = x_prompt.shape[0]
    mod = _ada(jnp.concatenate([c_prompt, c_sample], axis=0), w_ada, b_ada)
    mod = mod.reshape(DEPTH, -1, 3, D_MODEL)
    outs = []
    for x, sl in ((x_prompt, slice(0, nb_p)), (x_sample, slice(nb_p, None))):
        tabs = _rope_tables(x.shape[1])
        outs.append(_trunk(x, mod[0, sl], mod[1, sl], p, tabs))
    return tuple(outs)
```

```python
import functools

import numpy as np
import jax
import jax.numpy as jnp
from jax import lax
from jax.experimental import pallas as pl
from jax.experimental.pallas import tpu as pltpu

F32 = jnp.float32
BF16 = jnp.bfloat16

D_MODEL = 1024
DEPTH = 2
GRID_W = 64
EPS = 1e-6
MLA_HEADS = 8
QK_NOPE = 64
QK_ROPE = 32
V_DIM = 64
Q_LORA = 256
KV_LORA = 256
ROPE_THETA = 10000.0
MLA_WIDTH = MLA_HEADS * V_DIM
NA_HEADS = 8
NA_DIM = 64
NA_WIN_H = 8
NA_WIN_W = 16
NA_WIDTH = NA_HEADS * NA_DIM
ML_HEADS = 4
ML_INNER = 2 * D_MODEL
ML_DIM = ML_INNER // ML_HEADS
ML_CONV = 5
QKV_BLOCK = 4
ML_CHUNK = 128

HEAD_PAD = 128
MLA_QK = MLA_HEADS * HEAD_PAD
ROPE_HALF = QK_ROPE // 2
LANES = 128
BD_TILE = 256
N_BD = ML_INNER // BD_TILE
NEG_BIG = -1e30

VMEM_LIMIT = 56 * 1024 * 1024

_NT = (((1,), (1,)), ((), ()))
_TN = (((0,), (0,)), ((), ()))


def _cparams(sem):
    return pltpu.CompilerParams(dimension_semantics=sem, vmem_limit_bytes=VMEM_LIMIT)


def _silu(x):
    return x * jax.nn.sigmoid(x)


def _rms(x, g):
    return x * lax.rsqrt(jnp.mean(x * x, axis=-1, keepdims=True) + EPS) * g


def _ada_kernel(c_ref, w_ref, b_ref, o_ref):
    c = c_ref[...]
    o_ref[0] = jnp.dot(_silu(c).astype(BF16), w_ref[0].astype(BF16), preferred_element_type=F32) + b_ref[0]


def _ada(c_all, w_ada, b_ada):
    nb = c_all.shape[0]
    tn = 512
    return pl.pallas_call(
        _ada_kernel,
        grid=(DEPTH, 3 * D_MODEL // tn),
        in_specs=[pl.BlockSpec((nb, D_MODEL), lambda l, j: (0, 0)),
                  pl.BlockSpec((1, D_MODEL, tn), lambda l, j: (l, 0, j)),
                  pl.BlockSpec((1, 1, tn), lambda l, j: (l, 0, j))],
        out_specs=pl.BlockSpec((1, nb, tn), lambda l, j: (l, 0, j)),
        out_shape=jax.ShapeDtypeStruct((DEPTH, nb, 3 * D_MODEL), F32),
        compiler_params=_cparams(("arbitrary", "arbitrary")),
        name="ada",
    )(c_all, w_ada, b_ada.reshape(DEPTH, 1, 3 * D_MODEL))


def _in0_kernel(x_ref, mod_ref, g_ref, wa_ref, wb_ref, gq_ref, gkv_ref, wuq_ref, wukv_ref,
                rc_ref, rm_ref, rp_ref,
                q_ref, k_ref, v_ref, nq_ref, nk_ref, nv_ref, gate_ref):
    x = x_ref[0]
    shift = mod_ref[0, 0:1, :]
    scale = mod_ref[0, 1:2, :]
    h = _rms(x, g_ref[...]) * (1.0 + scale) + shift
    hb = h.astype(BF16)

    ua = jnp.dot(hb, wa_ref[...], preferred_element_type=F32)
    qn = _rms(ua[:, :Q_LORA], gq_ref[...]).astype(BF16)
    kvn = _rms(ua[:, Q_LORA:Q_LORA + KV_LORA], gkv_ref[...]).astype(BF16)
    kr = ua[:, Q_LORA + KV_LORA:]
    qf = jnp.dot(qn, wuq_ref[...], preferred_element_type=F32)
    kvf = jnp.dot(kvn, wukv_ref[...], preferred_element_type=F32)

    rc, rm, rp = rc_ref[...], rm_ref[...], rp_ref[...]

    def rope(t):
        return (t * rc + pltpu.roll(t, LANES - ROPE_HALF, 1) * rm + pltpu.roll(t, ROPE_HALF, 1) * rp)

    krr = rope(kr)
    qscale = (QK_NOPE + QK_ROPE) ** -0.5
    for hh in range(MLA_HEADS):
        sl = slice(hh * HEAD_PAD, (hh + 1) * HEAD_PAD)
        q_ref[0, :, sl] = (rope(qf[:, sl]) * qscale).astype(BF16)
        k_ref[0, :, sl] = (kvf[:, sl] + krr).astype(BF16)
    v_ref[0] = kvf[:, MLA_QK:].astype(BF16)

    ub = jnp.dot(hb, wb_ref[...], preferred_element_type=F32)
    nq_ref[0] = (ub[:, :NA_WIDTH] * (NA_DIM ** -0.5)).astype(BF16)
    nk_ref[0] = ub[:, NA_WIDTH:2 * NA_WIDTH].astype(BF16)
    nv_ref[0] = ub[:, 2 * NA_WIDTH:3 * NA_WIDTH].astype(BF16)
    gate_ref[0] = ub[:, 3 * NA_WIDTH:].astype(BF16)


def _in0(x, mod, g, wa, wb, gq, gkv, wuq, wukv, rc, rm, rp, tm=512):
    B, S, D = x.shape
    full = lambda a: pl.BlockSpec(a.shape, lambda b, i: (0,) * a.ndim)
    tok = lambda w: pl.BlockSpec((1, tm, w), lambda b, i: (b, i, 0))
    tab = pl.BlockSpec((tm, LANES), lambda b, i: (i, 0))
    widths = (MLA_QK, MLA_QK, MLA_WIDTH, NA_WIDTH, NA_WIDTH, NA_WIDTH, MLA_WIDTH + NA_WIDTH)
    return pl.pallas_call(
        _in0_kernel,
        grid=(B, S // tm),
        in_specs=[tok(D), pl.BlockSpec((1, 3, D), lambda b, i: (b, 0, 0)), full(g), full(wa), full(wb),
                  full(gq), full(gkv), full(wuq), full(wukv), tab, tab, tab],
        out_specs=[tok(w) for w in widths],
        out_shape=[jax.ShapeDtypeStruct((B, S, w), BF16) for w in widths],
        compiler_params=_cparams(("parallel", "parallel")),
        name="in0",
    )(x, mod, g, wa, wb, gq, gkv, wuq, wukv, rc, rm, rp)


def _mla_kernel(q_ref, k_ref, v_ref, o_ref):
    lane = lax.broadcasted_iota(jnp.int32, (q_ref.shape[1], 2 * V_DIM), 1)
    for pair in range(MLA_HEADS // 2):
        vs = slice(pair * 2 * V_DIM, (pair + 1) * 2 * V_DIM)
        v = v_ref[0, :, vs]
        outs = []
        for j in range(2):
            hh = 2 * pair + j
            sl = slice(hh * HEAD_PAD, (hh + 1) * HEAD_PAD)
            s = lax.dot_general(q_ref[0, :, sl], k_ref[0, :, sl], _NT, preferred_element_type=F32)
            m = jnp.max(s, axis=-1, keepdims=True)
            p = jnp.exp(s - m)
            l = jnp.sum(p, axis=-1, keepdims=True)
            outs.append(jnp.dot(p.astype(BF16), v, preferred_element_type=F32) / l)
        o_ref[0, :, vs] = jnp.where(lane < V_DIM, outs[0], outs[1]).astype(BF16)


def _mla(q, k, v, tq=256):
    B, S, _ = q.shape
    return pl.pallas_call(
        _mla_kernel,
        grid=(B, S // tq),
        in_specs=[pl.BlockSpec((1, tq, MLA_QK), lambda b, i: (b, i, 0)),
                  pl.BlockSpec((1, S, MLA_QK), lambda b, i: (b, 0, 0)),
                  pl.BlockSpec((1, S, MLA_WIDTH), lambda b, i: (b, 0, 0))],
        out_specs=pl.BlockSpec((1, tq, MLA_WIDTH), lambda b, i: (b, i, 0)),
        out_shape=jax.ShapeDtypeStruct((B, S, MLA_WIDTH), BF16),
        compiler_params=_cparams(("parallel", "arbitrary")),
        name="mla",
    )(q, k, v)


NA_GROUP = 4
NA_LANES = NA_GROUP * NA_DIM
NA_KEYS = NA_WIN_H * GRID_W
NA_UNROLL = 4


def _na_kernel(q_ref, k_ref, v_ref, bias_ref, o_ref, *, rows):
    rh = lax.broadcasted_iota(jnp.int32, (NA_LANES, NA_LANES), 0) // GRID_W
    lh = lax.broadcasted_iota(jnp.int32, (NA_LANES, NA_LANES), 1) // NA_DIM
    hmask = (rh == lh).astype(F32)

    def body(r, carry):
        r0 = jnp.clip(r - NA_WIN_H // 2, 0, rows - NA_WIN_H)
        d = r - r0
        qr = q_ref[0, pl.ds(pl.multiple_of(r * GRID_W, GRID_W), GRID_W), :].astype(F32)
        qs = (jnp.concatenate([qr] * NA_GROUP, axis=0) * hmask).astype(BF16)
        win = pl.ds(pl.multiple_of(r0 * GRID_W, GRID_W), NA_KEYS)
        s = lax.dot_general(qs, k_ref[0, win, :], _NT, preferred_element_type=F32) + bias_ref[0, d]
        m = jnp.max(s, axis=-1, keepdims=True)
        p = jnp.exp(s - m)
        l = jnp.sum(p, axis=-1, keepdims=True)
        o = jnp.dot(p.astype(BF16), v_ref[0, win, :], preferred_element_type=F32) / l * hmask
        out = o[0:GRID_W]
        for g in range(1, NA_GROUP):
            out = out + o[g * GRID_W:(g + 1) * GRID_W]
        o_ref[0, pl.ds(pl.multiple_of(r * GRID_W, GRID_W), GRID_W), :] = out.astype(BF16)
        return carry

    lax.fori_loop(0, rows, body, 0, unroll=NA_UNROLL)


def _natten(q, k, v, bias):
    B, S, _ = q.shape
    rows = S // GRID_W
    tok = pl.BlockSpec((1, S, NA_LANES), lambda g, b: (b, 0, g))
    return pl.pallas_call(
        functools.partial(_na_kernel, rows=rows),
        grid=(NA_HEADS // NA_GROUP, B),
        in_specs=[tok, tok, tok,
                  pl.BlockSpec((1, NA_WIN_H, NA_LANES, NA_KEYS), lambda g, b: (g, 0, 0, 0))],
        out_specs=tok,
        out_shape=jax.ShapeDtypeStruct((B, S, NA_WIDTH), BF16),
        compiler_params=_cparams(("parallel", "parallel")),
        name="natten",
    )(q, k, v, bias)


def _mid_kernel(om_ref, on_ref, gate_ref, x_ref, mod0_ref, mod1_ref, g_ref, wo_ref, wi_ref,
                x1_ref, xm_ref, z_ref):
    sg = _silu(gate_ref[0].astype(F32))
    o = jnp.concatenate([om_ref[0].astype(F32), on_ref[0].astype(F32)], axis=-1) * sg
    out = jnp.dot(o.astype(BF16), wo_ref[...], preferred_element_type=F32)
    x1 = x_ref[0] + mod0_ref[0, 2:3, :] * out
    x1_ref[0] = x1
    h = _rms(x1, g_ref[...]) * (1.0 + mod1_ref[0, 1:2, :]) + mod1_ref[0, 0:1, :]
    u = jnp.dot(h.astype(BF16), wi_ref[...], preferred_element_type=F32)
    xm_ref[0] = u[:, :ML_INNER].astype(BF16)
    z_ref[0] = u[:, ML_INNER:].astype(BF16)


def _mid(om, on, gate, x, mod0, mod1, g, wo, wi, tm=512):
    B, S, D = x.shape
    full = lambda a: pl.BlockSpec(a.shape, lambda b, i: (0,) * a.ndim)
    tok = lambda w: pl.BlockSpec((1, tm, w), lambda b, i: (b, i, 0))
    modspec = pl.BlockSpec((1, 3, D), lambda b, i: (b, 0, 0))
    return pl.pallas_call(
        _mid_kernel,
        grid=(B, S // tm),
        in_specs=[tok(MLA_WIDTH), tok(NA_WIDTH), tok(D), tok(D), modspec, modspec, full(g), full(wo), full(wi)],
        out_specs=[tok(D), tok(ML_INNER), tok(ML_INNER)],
        out_shape=[jax.ShapeDtypeStruct((B, S, D), F32),
                   jax.ShapeDtypeStruct((B, S, ML_INNER), BF16),
                   jax.ShapeDtypeStruct((B, S, ML_INNER), BF16)],
        compiler_params=_cparams(("parallel", "parallel")),
        name="mid",
    )(om, on, gate, x, mod0, mod1, g, wo, wi)


HALO = 16
PADR = 8


def _log_sigmoid(x):
    return jnp.minimum(x, 0.0) - jnp.log1p(jnp.exp(-jnp.abs(x)))


def _split3(x):
    hi = x.astype(BF16)
    r1 = x - hi.astype(F32)
    mid = r1.astype(BF16)
    lo = (r1 - mid.astype(F32)).astype(BF16)
    return hi, mid, lo


def _conv_kernel(xm_ref, xp_ref, xn_ref, cw_ref, cb_ref, bdqk_ref, bdv_ref, wg_ref, bg_ref, t1_ref, t2_ref,
                 q_ref, k_ref, v_ref, xc_ref, gc_ref, gr_ref, ext_ref, *, tm):
    i = pl.program_id(1)
    n = pl.num_programs(1)
    prev = xp_ref[0].astype(F32)[HALO - PADR:, :]
    nxt = xn_ref[0].astype(F32)[:PADR, :]
    ext_ref[0:PADR, :] = jnp.where(i > 0, prev, 0.0)
    ext_ref[PADR:PADR + tm, :] = xm_ref[0].astype(F32)
    ext_ref[PADR + tm:, :] = jnp.where(i < n - 1, nxt, 0.0)

    qscale = ML_DIM ** -0.5
    pre = jnp.zeros((tm, LANES), F32) + bg_ref[...]
    for c in range(N_BD):
        cs = slice(c * BD_TILE, (c + 1) * BD_TILE)
        y = jnp.zeros((tm, BD_TILE), F32) + cb_ref[:, cs]
        for j in range(ML_CONV):
            y = y + ext_ref[PADR - ML_CONV // 2 + j:PADR - ML_CONV // 2 + j + tm, cs] * cw_ref[j:j + 1, cs]
        xc = _silu(y)
        xcb = xc.astype(BF16)
        xc_ref[0, :, cs] = xcb
        qk = jnp.dot(xcb, bdqk_ref[c], preferred_element_type=F32)
        vv = jnp.dot(xm_ref[0, :, cs], bdv_ref[c], preferred_element_type=F32)
        qc, kc = qk[:, :BD_TILE], qk[:, BD_TILE:]
        qkv = jnp.concatenate([qc, kc, vv], axis=-1).astype(BF16)
        pre = pre + jnp.dot(qkv, wg_ref[c], preferred_element_type=F32)
        q_ref[0, :, cs] = (qc * qscale).astype(BF16)
        k_ref[0, :, cs] = kc.astype(BF16)
        v_ref[0, :, cs] = vv.astype(BF16)

    lf = _log_sigmoid(pre)
    cum_f = jnp.zeros((tm, LANES), F32)
    cum_b = jnp.zeros((tm, LANES), F32)
    for part in _split3(lf):
        cum_f = cum_f + jnp.dot(t1_ref[...], part, preferred_element_type=F32)
        cum_b = cum_b + jnp.dot(t2_ref[...], part, preferred_element_type=F32)
    qty = lax.broadcasted_iota(jnp.int32, (tm, LANES), 1) % 4
    g = jnp.where(qty == 1, cum_f, jnp.where(qty == 3, cum_b, pre))
    gc_ref[0] = g
    gr_ref[0] = g.T[:4 * ML_HEADS, :]


def _conv(xm, cw, cb, bdqk, bdv, wg, bg, t1, t2, tm=512):
    B, S, _ = xm.shape
    hb = tm // HALO
    full = lambda a: pl.BlockSpec(a.shape, lambda b, i: (0,) * a.ndim)
    tok = lambda w: pl.BlockSpec((1, tm, w), lambda b, i: (b, i, 0))
    return pl.pallas_call(
        functools.partial(_conv_kernel, tm=tm),
        grid=(B, S // tm),
        in_specs=[tok(ML_INNER),
                  pl.BlockSpec((1, HALO, ML_INNER), lambda b, i: (b, jnp.maximum(i * hb - 1, 0), 0)),
                  pl.BlockSpec((1, HALO, ML_INNER), lambda b, i: (b, jnp.minimum((i + 1) * hb, S // HALO - 1), 0)),
                  full(cw), full(cb), full(bdqk), full(bdv), full(wg), full(bg), full(t1), full(t2)],
        out_specs=[tok(ML_INNER), tok(ML_INNER), tok(ML_INNER), tok(ML_INNER), tok(LANES),
                   pl.BlockSpec((1, 4 * ML_HEADS, tm), lambda b, i: (b, 0, i))],
        out_shape=[jax.ShapeDtypeStruct((B, S, ML_INNER), BF16)] * 4
        + [jax.ShapeDtypeStruct((B, S, LANES), F32), jax.ShapeDtypeStruct((B, 4 * ML_HEADS, S), F32)],
        scratch_shapes=[pltpu.VMEM((tm + 2 * PADR, ML_INNER), F32)],
        compiler_params=_cparams(("parallel", "parallel")),
        name="conv",
    )(xm, xm, xm, cw, cb, bdqk, bdv, wg, bg, t1, t2)


def _mlstm_kernel(q_ref, k_ref, v_ref, gc_ref, gr_ref, gmh_ref, o_ref,
                  ctf_ref, ctb_ref, hf_ref, hb_ref, *, seq):
    L = ML_CHUNK
    nc = seq // L
    head = pl.program_id(1)
    ctf_ref[...] = jnp.zeros_like(ctf_ref)
    ctb_ref[...] = jnp.zeros_like(ctb_ref)
    lane = lax.broadcasted_iota(jnp.int32, (L, LANES), 1)
    ri = lax.broadcasted_iota(jnp.int32, (L, L), 0)
    ci = lax.broadcasted_iota(jnp.int32, (L, L), 1)

    def chunk(c, ct_ref, nvec, m, qi, qb, lower):
        rows = pl.ds(pl.multiple_of(c * L, L), L)
        qc, kc, vc = q_ref[0, rows, :], k_ref[0, rows, :], v_ref[0, rows, :]
        gcol = gc_ref[0, rows, :]

        def col(j):
            return jnp.sum(jnp.where(lane == head * 4 + j, gcol, 0.0), axis=1, keepdims=True)

        i_col, b_col = col(qi), col(qb)
        i_row = gr_ref[0, 0, qi:qi + 1, rows]
        b_row = gr_ref[0, 0, qb:qb + 1, rows]
        mask = (ci <= ri) if lower else (ci >= ri)
        dlog = jnp.where(mask, b_col - b_row + i_row, -jnp.inf)
        inter = b_col + m
        m_t = jnp.maximum(inter, jnp.max(dlog, axis=-1, keepdims=True))
        dmat = jnp.exp(dlog - m_t)
        w_inter = jnp.exp(inter - m_t)
        s = lax.dot_general(qc, kc, _NT, preferred_element_type=F32) * dmat
        num = (w_inter * jnp.dot(qc, ct_ref[...].astype(BF16), preferred_element_type=F32)
               + jnp.dot(s.astype(BF16), vc, preferred_element_type=F32))
        qn = jnp.sum(qc.astype(F32) * nvec, axis=-1, keepdims=True)
        den = w_inter * qn + jnp.sum(s, axis=-1, keepdims=True)
        hh = num / jnp.maximum(jnp.abs(den), jnp.exp(-m_t))
        b_last = b_col[L - 1:L, :] if lower else b_col[0:1, :]
        g_col = b_last - b_col + i_col
        m_new = jnp.maximum(b_last + m, jnp.max(g_col, axis=0, keepdims=True))
        w_old = jnp.exp(b_last + m - m_new)
        w_s = jnp.exp(g_col - m_new)
        wv = (w_s * vc.astype(F32)).astype(BF16)
        ct_ref[...] = w_old * ct_ref[...] + lax.dot_general(kc, wv, _TN, preferred_element_type=F32)
        nvec = w_old * nvec + jnp.sum(w_s * kc.astype(F32), axis=0, keepdims=True)
        return rows, hh, nvec, m_new

    def body(c, carry):
        nf, mf, nb, mb = carry
        rows_f, hf, nf, mf = chunk(c, ctf_ref, nf, mf, 0, 1, True)
        rows_b, hb, nb, mb = chunk(nc - 1 - c, ctb_ref, nb, mb, 2, 3, False)
        hf_ref[rows_f, :] = hf
        hb_ref[rows_b, :] = hb
        return nf, mf, nb, mb

    z_n = jnp.zeros((1, ML_DIM), F32)
    z_m = jnp.zeros((1, 1), F32)
    lax.fori_loop(0, nc, body, (z_n, z_m, z_n, z_m))

    te = 256
    for t in range(seq // te):
        rs = slice(t * te, (t + 1) * te)
        hs = hf_ref[rs, :] + hb_ref[rs, :]
        mu = jnp.mean(hs, axis=-1, keepdims=True)
        var = jnp.mean(jnp.square(hs - mu), axis=-1, keepdims=True)
        o_ref[0, rs, :] = ((hs - mu) * lax.rsqrt(var + EPS) * gmh_ref[...]).astype(BF16)


def _mlstm(q, k, v, gc, gr, gmh):
    B, S, _ = q.shape
    tok = pl.BlockSpec((1, S, ML_DIM), lambda b, h: (b, 0, h))
    return pl.pallas_call(
        functools.partial(_mlstm_kernel, seq=S),
        grid=(B, ML_HEADS),
        in_specs=[tok, tok, tok,
                  pl.BlockSpec((1, S, LANES), lambda b, h: (b, 0, 0)),
                  pl.BlockSpec((1, 1, 4, S), lambda b, h: (b, h, 0, 0)),
                  pl.BlockSpec((1, ML_DIM), lambda b, h: (0, h))],
        out_specs=tok,
        out_shape=jax.ShapeDtypeStruct((B, S, ML_INNER), BF16),
        scratch_shapes=[pltpu.VMEM((ML_DIM, ML_DIM), F32), pltpu.VMEM((ML_DIM, ML_DIM), F32),
                        pltpu.VMEM((S, ML_DIM), F32), pltpu.VMEM((S, ML_DIM), F32)],
        compiler_params=_cparams(("parallel", "arbitrary")),
        name="mlstm",
    )(q, k, v, gc, gr, gmh)


def _out_kernel(hn_ref, xc_ref, z_ref, x1_ref, mod1_ref, skip_ref, wo_ref, gf_ref, y_ref):
    o = (hn_ref[0].astype(F32) + skip_ref[...] * xc_ref[0].astype(F32)) * _silu(z_ref[0].astype(F32))
    out = jnp.dot(o.astype(BF16), wo_ref[...], preferred_element_type=F32)
    x2 = x1_ref[0] + mod1_ref[0, 2:3, :] * out
    y_ref[0] = _rms(x2, gf_ref[...])


def _out(hn, xc, z, x1, mod1, skip, wo, gf, tm=512):
    B, S, D = x1.shape
    full = lambda a: pl.BlockSpec(a.shape, lambda b, i: (0,) * a.ndim)
    tok = lambda w: pl.BlockSpec((1, tm, w), lambda b, i: (b, i, 0))
    return pl.pallas_call(
        _out_kernel,
        grid=(B, S // tm),
        in_specs=[tok(ML_INNER), tok(ML_INNER), tok(ML_INNER), tok(D),
                  pl.BlockSpec((1, 3, D), lambda b, i: (b, 0, 0)), full(skip), full(wo), full(gf)],
        out_specs=tok(D),
        out_shape=jax.ShapeDtypeStruct((B, S, D), F32),
        compiler_params=_cparams(("parallel", "parallel")),
        name="out",
    )(hn, xc, z, x1, mod1, skip, wo, gf)


def _rope_tables(S):
    pos = jnp.arange(S, dtype=F32)
    inv = 1.0 / (ROPE_THETA ** (jnp.arange(0, QK_ROPE, 2, dtype=F32) / QK_ROPE))
    ang = pos[:, None] * inv[None, :]
    cos, sin = jnp.cos(ang), jnp.sin(ang)
    ones = jnp.ones((S, QK_NOPE), F32)
    z16 = jnp.zeros((S, ROPE_HALF), F32)
    z32 = jnp.zeros((S, HEAD_PAD - QK_NOPE - QK_ROPE), F32)
    z64 = jnp.zeros((S, QK_NOPE), F32)
    rc = jnp.concatenate([ones, cos, cos, z32], axis=-1)
    rm = jnp.concatenate([z64, -sin, z16, z32], axis=-1)
    rp = jnp.concatenate([z64, z16, sin, z32], axis=-1)
    return rc, rm, rp


def _na_bias(rpb):
    col = np.arange(GRID_W)
    col_start = np.clip(col - NA_WIN_W // 2, 0, GRID_W - NA_WIN_W)
    col_in = (col[None, :] >= col_start[:, None]) & (col[None, :] < col_start[:, None] + NA_WIN_W)
    period = 2 * GRID_W
    n_r = 2 * NA_WIN_H - 1
    vec = jnp.zeros((NA_HEADS, n_r, period), F32)
    vec = vec.at[..., :NA_WIN_W].set(rpb[..., NA_WIN_W - 1:])
    vec = vec.at[..., period - (NA_WIN_W - 1):].set(rpb[..., :NA_WIN_W - 1])
    toep = jnp.tile(vec, (1, 1, GRID_W))[..., :GRID_W * (period - 1)]
    toep = toep.reshape(NA_HEADS, n_r, GRID_W, period - 1)[..., :GRID_W]
    b = jnp.stack([toep[:, NA_WIN_H - 1 - d:2 * NA_WIN_H - 1 - d] for d in range(NA_WIN_H)], axis=1)
    b = jnp.where(col_in[None, None, None], b, NEG_BIG)
    b = b.transpose(0, 1, 3, 2, 4).reshape(NA_HEADS, NA_WIN_H, GRID_W, NA_KEYS)
    b = b.reshape(NA_HEADS // NA_GROUP, NA_GROUP, NA_WIN_H, GRID_W, NA_KEYS).transpose(0, 2, 1, 3, 4)
    return b.reshape(NA_HEADS // NA_GROUP, NA_WIN_H, NA_LANES, NA_KEYS).astype(F32)


def _block_diag(w):
    per = BD_TILE // QKV_BLOCK
    wt = w.transpose(0, 2, 1).reshape(N_BD, per, QKV_BLOCK, QKV_BLOCK)
    eye = jnp.eye(per, dtype=w.dtype)
    return jnp.einsum('caio,ab->caibo', wt, eye).reshape(N_BD, BD_TILE, BD_TILE)


def _tri(tm, lower):
    t = np.arange(tm)
    same = (t[:, None] // ML_CHUNK) == (t[None, :] // ML_CHUNK)
    tri = (t[None, :] <= t[:, None]) if lower else (t[None, :] >= t[:, None])
    return jnp.asarray(same & tri, dtype=BF16)


def _prep(g_norm, g_final, w_in0, g_qlat, g_kvlat, w_uq, w_ukv, na_rpb, w_out0,
          w_in1, conv_w, conv_b, w_q, w_k, w_v, w_gate, b_gate, g_mh, skip, w_out1):
    p = {}
    w0 = w_in0[0]
    o = np.cumsum((Q_LORA, KV_LORA, QK_ROPE, NA_WIDTH, NA_WIDTH, NA_WIDTH))
    kr = jnp.zeros((D_MODEL, HEAD_PAD), F32).at[:, QK_NOPE:QK_NOPE + QK_ROPE].set(w0[:, o[1]:o[2]])
    p['wa'] = jnp.concatenate([w0[:, :o[1]], kr], axis=-1).astype(BF16)
    p['wb'] = w0[:, o[2]:].astype(BF16)
    wq = w_uq[0].reshape(Q_LORA, MLA_HEADS, QK_NOPE + QK_ROPE)
    p['wuq'] = jnp.pad(wq, ((0, 0), (0, 0), (0, HEAD_PAD - QK_NOPE - QK_ROPE))).reshape(Q_LORA, MLA_QK).astype(BF16)
    wkv = w_ukv[0].reshape(KV_LORA, MLA_HEADS, QK_NOPE + V_DIM)
    wk = jnp.pad(wkv[:, :, :QK_NOPE], ((0, 0), (0, 0), (0, HEAD_PAD - QK_NOPE))).reshape(KV_LORA, MLA_QK)
    wv = wkv[:, :, QK_NOPE:].reshape(KV_LORA, MLA_WIDTH)
    p['wukv'] = jnp.concatenate([wk, wv], axis=-1).astype(BF16)
    p['gq'] = g_qlat[0].reshape(1, Q_LORA)
    p['gkv'] = g_kvlat[0].reshape(1, KV_LORA)
    p['g0'] = g_norm[0].reshape(1, D_MODEL)
    p['g1'] = g_norm[1].reshape(1, D_MODEL)
    p['gf'] = g_final.reshape(1, D_MODEL)
    p['bias'] = _na_bias(na_rpb[0])
    p['wo0'] = w_out0[0].astype(BF16)
    p['wi1'] = w_in1[0].astype(BF16)
    p['cw'] = conv_w[0]
    p['cb'] = conv_b[0].reshape(1, ML_INNER)
    p['bdqk'] = jnp.concatenate([_block_diag(w_q[0]), _block_diag(w_k[0])], axis=-1).astype(BF16)
    p['bdv'] = _block_diag(w_v[0]).astype(BF16)
    wg = w_gate[0].reshape(3, N_BD, BD_TILE, 4, ML_HEADS).transpose(1, 0, 2, 4, 3)
    wg = wg.reshape(N_BD, 3 * BD_TILE, 4 * ML_HEADS)
    p['wg'] = jnp.pad(wg, ((0, 0), (0, 0), (0, LANES - 4 * ML_HEADS))).astype(BF16)
    bg = b_gate[0].reshape(4, ML_HEADS).T.reshape(1, 4 * ML_HEADS)
    p['bg'] = jnp.pad(bg, ((0, 0), (0, LANES - 4 * ML_HEADS)))
    p['gmh'] = g_mh[0].reshape(1, ML_INNER)
    p['skip'] = skip[0].reshape(1, ML_INNER)
    p['wo1'] = w_out1[0].astype(BF16)
    return p


def _trunk(x, mod0, mod1, p, tabs, tm=512):
    B, S, _ = x.shape
    rc, rm, rp = tabs
    q, k, v, nq, nk, nv, gate = _in0(x, mod0, p['g0'], p['wa'], p['wb'], p['gq'], p['gkv'],
                                      p['wuq'], p['wukv'], rc, rm, rp, tm=tm)
    o_mla = _mla(q, k, v)
    o_na = _natten(nq, nk, nv, p['bias'])
    x1, xm, z = _mid(o_mla, o_na, gate, x, mod0, mod1, p['g1'], p['wo0'], p['wi1'], tm=tm)
    t1, t2 = _tri(tm, True), _tri(tm, False)
    mq, mk, mv, xc, gc, gr = _conv(xm, p['cw'], p['cb'], p['bdqk'], p['bdv'], p['wg'], p['bg'], t1, t2, tm=tm)
    hn = _mlstm(mq, mk, mv, gc, gr.reshape(B, ML_HEADS, 4, S), p['gmh'])
    return _out(hn, xc, z, x1, mod1, p['skip'], p['wo1'], p['gf'], tm=tm)


def kernel(x_prompt, x_sample, c_prompt, c_sample, g_norm, w_ada, b_ada, g_final, w_in0, g_qlat, g_kvlat, w_uq, w_ukv, na_rpb, w_out0, w_in1, conv_w, conv_b, w_q, w_k, w_v, w_gate, b_gate, g_mh, skip, w_out1):
    p = _prep(g_norm, g_final, w_in0, g_qlat, g_kvlat, w_uq, w_ukv, na_rpb, w_out0,
              w_in1, conv_w, conv_b, w_q, w_k, w_v, w_gate, b_gate, g_mh, skip, w_out1)
    nb_p = x_prompt.shape[0]
    mod = _ada(jnp.concatenate([c_prompt, c_sample], axis=0), w_ada, b_ada)
    mod = mod.reshape(DEPTH, -1, 3, D_MODEL)
    outs = []
    for x, sl in ((x_prompt, slice(0, nb_p)), (x_sample, slice(nb_p, None))):
        tabs = _rope_tables(x.shape[1])
        outs.append(_trunk(x, mod[0, sl], mod[1, sl], p, tabs))
    return tuple(outs)
```

```python
import functools

import numpy as np
import jax
import jax.numpy as jnp
from jax import lax
from jax.experimental import pallas as pl
from jax.experimental.pallas import tpu as pltpu

F32 = jnp.float32
BF16 = jnp.bfloat16

D_MODEL = 1024
DEPTH = 2
GRID_W = 64
EPS = 1e-6
MLA_HEADS = 8
QK_NOPE = 64
QK_ROPE = 32
V_DIM = 64
Q_LORA = 256
KV_LORA = 256
ROPE_THETA = 10000.0
MLA_WIDTH = MLA_HEADS * V_DIM
NA_HEADS = 8
NA_DIM = 64
NA_WIN_H = 8
NA_WIN_W = 16
NA_WIDTH = NA_HEADS * NA_DIM
ML_HEADS = 4
ML_INNER = 2 * D_MODEL
ML_DIM = ML_INNER // ML_HEADS
ML_CONV = 5
QKV_BLOCK = 4
ML_L = 256

HEAD_PAD = 128
MLA_QK = MLA_HEADS * HEAD_PAD
ROPE_HALF = QK_ROPE // 2
LANES = 128
BD_TILE = 256
N_BD = ML_INNER // BD_TILE
NEG_BIG = -1e30

VMEM_LIMIT = 56 * 1024 * 1024

_NT = (((1,), (1,)), ((), ()))
_TN = (((0,), (0,)), ((), ()))


def _cparams(sem):
    return pltpu.CompilerParams(dimension_semantics=sem, vmem_limit_bytes=VMEM_LIMIT)


def _silu(x):
    return x * jax.nn.sigmoid(x)


def _rms(x, g):
    return x * lax.rsqrt(jnp.mean(x * x, axis=-1, keepdims=True) + EPS) * g


def _ada_kernel(c_ref, w_ref, b_ref, o_ref):
    c = c_ref[...]
    o_ref[0] = jnp.dot(_silu(c).astype(BF16), w_ref[0].astype(BF16), preferred_element_type=F32) + b_ref[0]


def _ada(c_all, w_ada, b_ada):
    nb = c_all.shape[0]
    tn = 512
    return pl.pallas_call(
        _ada_kernel,
        grid=(DEPTH, 3 * D_MODEL // tn),
        in_specs=[pl.BlockSpec((nb, D_MODEL), lambda l, j: (0, 0)),
                  pl.BlockSpec((1, D_MODEL, tn), lambda l, j: (l, 0, j)),
                  pl.BlockSpec((1, 1, tn), lambda l, j: (l, 0, j))],
        out_specs=pl.BlockSpec((1, nb, tn), lambda l, j: (l, 0, j)),
        out_shape=jax.ShapeDtypeStruct((DEPTH, nb, 3 * D_MODEL), F32),
        compiler_params=_cparams(("arbitrary", "arbitrary")),
        name="ada",
    )(c_all, w_ada, b_ada.reshape(DEPTH, 1, 3 * D_MODEL))


def _in0_kernel(x_ref, mod_ref, g_ref, wa_ref, wb_ref, gq_ref, gkv_ref, wuq_ref, wukv_ref,
                rc_ref, rm_ref, rp_ref,
                q_ref, k_ref, v_ref, nq_ref, nk_ref, nv_ref, gate_ref):
    x = x_ref[0]
    shift = mod_ref[0, 0:1, :]
    scale = mod_ref[0, 1:2, :]
    h = _rms(x, g_ref[...]) * (1.0 + scale) + shift
    hb = h.astype(BF16)

    ua = jnp.dot(hb, wa_ref[...], preferred_element_type=F32)
    qn = _rms(ua[:, :Q_LORA], gq_ref[...]).astype(BF16)
    kvn = _rms(ua[:, Q_LORA:Q_LORA + KV_LORA], gkv_ref[...]).astype(BF16)
    kr = ua[:, Q_LORA + KV_LORA:]
    qf = jnp.dot(qn, wuq_ref[...], preferred_element_type=F32)
    kvf = jnp.dot(kvn, wukv_ref[...], preferred_element_type=F32)

    rc, rm, rp = rc_ref[...], rm_ref[...], rp_ref[...]

    def rope(t):
        return (t * rc + pltpu.roll(t, LANES - ROPE_HALF, 1) * rm + pltpu.roll(t, ROPE_HALF, 1) * rp)

    krr = rope(kr)
    qscale = (QK_NOPE + QK_ROPE) ** -0.5
    for hh in range(MLA_HEADS):
        sl = slice(hh * HEAD_PAD, (hh + 1) * HEAD_PAD)
        q_ref[0, :, sl] = (rope(qf[:, sl]) * qscale).astype(BF16)
        k_ref[0, :, sl] = (kvf[:, sl] + krr).astype(BF16)
    v_ref[0] = kvf[:, MLA_QK:].astype(BF16)

    ub = jnp.dot(hb, wb_ref[...], preferred_element_type=F32)
    nq_ref[0] = (ub[:, :NA_WIDTH] * (NA_DIM ** -0.5)).astype(BF16)
    nk_ref[0] = ub[:, NA_WIDTH:2 * NA_WIDTH].astype(BF16)
    nv_ref[0] = ub[:, 2 * NA_WIDTH:3 * NA_WIDTH].astype(BF16)
    gate_ref[0] = ub[:, 3 * NA_WIDTH:].astype(BF16)


def _in0(x, mod, g, wa, wb, gq, gkv, wuq, wukv, rc, rm, rp, tm=512):
    B, S, D = x.shape
    full = lambda a: pl.BlockSpec(a.shape, lambda b, i: (0,) * a.ndim)
    tok = lambda w: pl.BlockSpec((1, tm, w), lambda b, i: (b, i, 0))
    tab = pl.BlockSpec((tm, LANES), lambda b, i: (i, 0))
    widths = (MLA_QK, MLA_QK, MLA_WIDTH, NA_WIDTH, NA_WIDTH, NA_WIDTH, MLA_WIDTH + NA_WIDTH)
    return pl.pallas_call(
        _in0_kernel,
        grid=(B, S // tm),
        in_specs=[tok(D), pl.BlockSpec((1, 3, D), lambda b, i: (b, 0, 0)), full(g), full(wa), full(wb),
                  full(gq), full(gkv), full(wuq), full(wukv), tab, tab, tab],
        out_specs=[tok(w) for w in widths],
        out_shape=[jax.ShapeDtypeStruct((B, S, w), BF16) for w in widths],
        compiler_params=_cparams(("parallel", "parallel")),
        name="in0",
    )(x, mod, g, wa, wb, gq, gkv, wuq, wukv, rc, rm, rp)


def _mla_kernel(q_ref, k_ref, v_ref, o_ref):
    lane = lax.broadcasted_iota(jnp.int32, (q_ref.shape[1], 2 * V_DIM), 1)
    for pair in range(MLA_HEADS // 2):
        vs = slice(pair * 2 * V_DIM, (pair + 1) * 2 * V_DIM)
        v = v_ref[0, :, vs]
        outs = []
        for j in range(2):
            hh = 2 * pair + j
            sl = slice(hh * HEAD_PAD, (hh + 1) * HEAD_PAD)
            s = lax.dot_general(q_ref[0, :, sl], k_ref[0, :, sl], _NT, preferred_element_type=F32)
            m = jnp.max(s, axis=-1, keepdims=True)
            p = jnp.exp(s - m)
            l = jnp.sum(p, axis=-1, keepdims=True)
            outs.append(jnp.dot(p.astype(BF16), v, preferred_element_type=F32) / l)
        o_ref[0, :, vs] = jnp.where(lane < V_DIM, outs[0], outs[1]).astype(BF16)


def _mla(q, k, v, tq=256):
    B, S, _ = q.shape
    return pl.pallas_call(
        _mla_kernel,
        grid=(B, S // tq),
        in_specs=[pl.BlockSpec((1, tq, MLA_QK), lambda b, i: (b, i, 0)),
                  pl.BlockSpec((1, S, MLA_QK), lambda b, i: (b, 0, 0)),
                  pl.BlockSpec((1, S, MLA_WIDTH), lambda b, i: (b, 0, 0))],
        out_specs=pl.BlockSpec((1, tq, MLA_WIDTH), lambda b, i: (b, i, 0)),
        out_shape=jax.ShapeDtypeStruct((B, S, MLA_WIDTH), BF16),
        compiler_params=_cparams(("parallel", "arbitrary")),
        name="mla",
    )(q, k, v)


NA_GROUP = 4
NA_LANES = NA_GROUP * NA_DIM
NA_KEYS = NA_WIN_H * GRID_W
NA_UNROLL = 4


def _na_kernel(q_ref, k_ref, v_ref, bias_ref, o_ref, *, rows):
    rh = lax.broadcasted_iota(jnp.int32, (NA_LANES, NA_LANES), 0) // GRID_W
    lh = lax.broadcasted_iota(jnp.int32, (NA_LANES, NA_LANES), 1) // NA_DIM
    hmask = (rh == lh).astype(F32)

    def body(r, carry):
        r0 = jnp.clip(r - NA_WIN_H // 2, 0, rows - NA_WIN_H)
        d = r - r0
        qr = q_ref[0, pl.ds(pl.multiple_of(r * GRID_W, GRID_W), GRID_W), :].astype(F32)
        qs = (jnp.concatenate([qr] * NA_GROUP, axis=0) * hmask).astype(BF16)
        win = pl.ds(pl.multiple_of(r0 * GRID_W, GRID_W), NA_KEYS)
        s = lax.dot_general(qs, k_ref[0, win, :], _NT, preferred_element_type=F32) + bias_ref[0, d]
        m = jnp.max(s, axis=-1, keepdims=True)
        p = jnp.exp(s - m)
        l = jnp.sum(p, axis=-1, keepdims=True)
        o = jnp.dot(p.astype(BF16), v_ref[0, win, :], preferred_element_type=F32) / l * hmask
        out = o[0:GRID_W]
        for g in range(1, NA_GROUP):
            out = out + o[g * GRID_W:(g + 1) * GRID_W]
        o_ref[0, pl.ds(pl.multiple_of(r * GRID_W, GRID_W), GRID_W), :] = out.astype(BF16)
        return carry

    lax.fori_loop(0, rows, body, 0, unroll=NA_UNROLL)


def _natten(q, k, v, bias):
    B, S, _ = q.shape
    rows = S // GRID_W
    tok = pl.BlockSpec((1, S, NA_LANES), lambda g, b: (b, 0, g))
    return pl.pallas_call(
        functools.partial(_na_kernel, rows=rows),
        grid=(NA_HEADS // NA_GROUP, B),
        in_specs=[tok, tok, tok,
                  pl.BlockSpec((1, NA_WIN_H, NA_LANES, NA_KEYS), lambda g, b: (g, 0, 0, 0))],
        out_specs=tok,
        out_shape=jax.ShapeDtypeStruct((B, S, NA_WIDTH), BF16),
        compiler_params=_cparams(("parallel", "parallel")),
        name="natten",
    )(q, k, v, bias)


def _mid_kernel(om_ref, on_ref, gate_ref, x_ref, mod0_ref, mod1_ref, g_ref, wo_ref, wi_ref,
                x1_ref, xm_ref, z_ref):
    sg = _silu(gate_ref[0].astype(F32))
    o = jnp.concatenate([om_ref[0].astype(F32), on_ref[0].astype(F32)], axis=-1) * sg
    out = jnp.dot(o.astype(BF16), wo_ref[...], preferred_element_type=F32)
    x1 = x_ref[0] + mod0_ref[0, 2:3, :] * out
    x1_ref[0] = x1
    h = _rms(x1, g_ref[...]) * (1.0 + mod1_ref[0, 1:2, :]) + mod1_ref[0, 0:1, :]
    u = jnp.dot(h.astype(BF16), wi_ref[...], preferred_element_type=F32)
    xm_ref[0] = u[:, :ML_INNER].astype(BF16)
    z_ref[0] = u[:, ML_INNER:].astype(BF16)


def _mid(om, on, gate, x, mod0, mod1, g, wo, wi, tm=512):
    B, S, D = x.shape
    full = lambda a: pl.BlockSpec(a.shape, lambda b, i: (0,) * a.ndim)
    tok = lambda w: pl.BlockSpec((1, tm, w), lambda b, i: (b, i, 0))
    modspec = pl.BlockSpec((1, 3, D), lambda b, i: (b, 0, 0))
    return pl.pallas_call(
        _mid_kernel,
        grid=(B, S // tm),
        in_specs=[tok(MLA_WIDTH), tok(NA_WIDTH), tok(D), tok(D), modspec, modspec, full(g), full(wo), full(wi)],
        out_specs=[tok(D), tok(ML_INNER), tok(ML_INNER)],
        out_shape=[jax.ShapeDtypeStruct((B, S, D), F32),
                   jax.ShapeDtypeStruct((B, S, ML_INNER), BF16),
                   jax.ShapeDtypeStruct((B, S, ML_INNER), BF16)],
        compiler_params=_cparams(("parallel", "parallel")),
        name="mid",
    )(om, on, gate, x, mod0, mod1, g, wo, wi)


HALO = 16
SHIFT_ROWS = 128
SHIFT_WIN = SHIFT_ROWS + 2 * HALO
PADR = 8
MXU_TAPS = (ML_CONV // 2 - 1, ML_CONV // 2 + 1)
VALU_TAPS = tuple(j for j in range(ML_CONV) if j not in MXU_TAPS)


def _log_sigmoid(x):
    return jnp.minimum(x, 0.0) - jnp.log1p(jnp.exp(-jnp.abs(x)))


def _split3(x):
    hi = x.astype(BF16)
    r1 = x - hi.astype(F32)
    mid = r1.astype(BF16)
    lo = (r1 - mid.astype(F32)).astype(BF16)
    return hi, mid, lo


def _conv_kernel(xm_ref, xp_ref, xn_ref, sh_ref, cw_ref, cb_ref, bdqk_ref, bdv_ref, wg_ref, bg_ref, t1_ref, t2_ref,
                 q_ref, k_ref, v_ref, xc_ref, gc_ref, gr_ref, ext_ref, *, tm):
    i = pl.program_id(1)
    n = pl.num_programs(1)
    ext_ref[0:PADR, :] = jnp.where(i > 0, xp_ref[0].astype(F32)[HALO - PADR:, :], 0.0)
    ext_ref[PADR:PADR + tm, :] = xm_ref[0].astype(F32)
    ext_ref[PADR + tm:, :] = jnp.where(i < n - 1, xn_ref[0].astype(F32)[:PADR, :], 0.0)
    zero_halo = jnp.zeros((HALO, BD_TILE), BF16)
    qscale = ML_DIM ** -0.5
    pre = jnp.zeros((tm, LANES), F32) + bg_ref[...]
    for c in range(N_BD):
        cs = slice(c * BD_TILE, (c + 1) * BD_TILE)
        prev = jnp.where(i > 0, xp_ref[0, :, cs], zero_halo)
        nxt = jnp.where(i < n - 1, xn_ref[0, :, cs], zero_halo)
        ys = []
        for rb in range(tm // SHIFT_ROWS):
            lo, hi = rb * SHIFT_ROWS - HALO, (rb + 1) * SHIFT_ROWS + HALO
            parts = [prev] if lo < 0 else []
            parts.append(xm_ref[0, max(lo, 0):min(hi, tm), cs])
            if hi > tm:
                parts.append(nxt)
            win = jnp.concatenate(parts, axis=0) if len(parts) > 1 else parts[0]
            sh = jnp.dot(sh_ref[...], win, preferred_element_type=F32)
            y = cb_ref[:, cs]
            for t, j in enumerate(MXU_TAPS):
                y = y + sh[t * SHIFT_ROWS:(t + 1) * SHIFT_ROWS] * cw_ref[j:j + 1, cs]
            for j in VALU_TAPS:
                r0 = PADR + rb * SHIFT_ROWS + j - ML_CONV // 2
                y = y + ext_ref[r0:r0 + SHIFT_ROWS, cs] * cw_ref[j:j + 1, cs]
            ys.append(y)
        xc = _silu(jnp.concatenate(ys, axis=0))
        xcb = xc.astype(BF16)
        xc_ref[0, :, cs] = xcb
        qk = jnp.dot(xcb, bdqk_ref[c], preferred_element_type=F32)
        vv = jnp.dot(xm_ref[0, :, cs], bdv_ref[c], preferred_element_type=F32)
        qc, kc = qk[:, :BD_TILE], qk[:, BD_TILE:]
        qkv = jnp.concatenate([qc, kc, vv], axis=-1).astype(BF16)
        pre = pre + jnp.dot(qkv, wg_ref[c], preferred_element_type=F32)
        q_ref[0, :, cs] = (qc * qscale).astype(BF16)
        k_ref[0, :, cs] = kc.astype(BF16)
        v_ref[0, :, cs] = vv.astype(BF16)

    lf = _log_sigmoid(pre)
    cum_f = jnp.zeros((tm, LANES), F32)
    cum_b = jnp.zeros((tm, LANES), F32)
    for part in _split3(lf):
        cum_f = cum_f + jnp.dot(t1_ref[...], part, preferred_element_type=F32)
        cum_b = cum_b + jnp.dot(t2_ref[...], part, preferred_element_type=F32)
    qty = lax.broadcasted_iota(jnp.int32, (tm, LANES), 1) % 4
    g = jnp.where(qty == 1, cum_f, jnp.where(qty == 3, cum_b, pre))
    gc_ref[0] = g
    gr_ref[0] = g.T[:4 * ML_HEADS, :]


def _shift_matrix():
    m = np.zeros((len(MXU_TAPS) * SHIFT_ROWS, SHIFT_WIN), np.float32)
    r = np.arange(SHIFT_ROWS)
    for t, j in enumerate(MXU_TAPS):
        m[t * SHIFT_ROWS + r, r + HALO + j - ML_CONV // 2] = 1.0
    return jnp.asarray(m, dtype=BF16)


def _conv(xm, cw, cb, bdqk, bdv, wg, bg, t1, t2, tm=512):
    sh = _shift_matrix()
    B, S, _ = xm.shape
    hb = tm // HALO
    full = lambda a: pl.BlockSpec(a.shape, lambda b, i: (0,) * a.ndim)
    tok = lambda w: pl.BlockSpec((1, tm, w), lambda b, i: (b, i, 0))
    return pl.pallas_call(
        functools.partial(_conv_kernel, tm=tm),
        grid=(B, S // tm),
        in_specs=[tok(ML_INNER),
                  pl.BlockSpec((1, HALO, ML_INNER), lambda b, i: (b, jnp.maximum(i * hb - 1, 0), 0)),
                  pl.BlockSpec((1, HALO, ML_INNER), lambda b, i: (b, jnp.minimum((i + 1) * hb, S // HALO - 1), 0)),
                  full(sh), full(cw), full(cb), full(bdqk), full(bdv), full(wg), full(bg), full(t1), full(t2)],
        out_specs=[tok(ML_INNER), tok(ML_INNER), tok(ML_INNER), tok(ML_INNER), tok(LANES),
                   pl.BlockSpec((1, 4 * ML_HEADS, tm), lambda b, i: (b, 0, i))],
        out_shape=[jax.ShapeDtypeStruct((B, S, ML_INNER), BF16)] * 4
        + [jax.ShapeDtypeStruct((B, S, LANES), F32), jax.ShapeDtypeStruct((B, 4 * ML_HEADS, S), F32)],
        scratch_shapes=[pltpu.VMEM((tm + 2 * PADR, ML_INNER), F32)],
        compiler_params=_cparams(("parallel", "parallel")),
        name="conv",
    )(xm, xm, xm, sh, cw, cb, bdqk, bdv, wg, bg, t1, t2)


ML_UNROLL = 2


def _mlstm_kernel(q_ref, k_ref, v_ref, gc_ref, gr_ref, gmh_ref, o_ref,
                  cf_ref, cb_ref, sf_ref, sb_ref, nst_ref, mf_ref, mb_ref, bcol_ref, *, seq):
    L = ML_L
    nc = seq // L
    head = pl.program_id(1)
    cf_ref[...] = jnp.zeros_like(cf_ref)
    cb_ref[...] = jnp.zeros_like(cb_ref)
    lane = lax.broadcasted_iota(jnp.int32, (L, LANES), 1)

    def wide(x, n):
        return jnp.concatenate([x] * n, axis=1)

    def state_step(c, d, ct_ref, st_ref, mst_ref, nvec, m, qi, qb, lower):
        rows = pl.ds(pl.multiple_of(c * L, L), L)
        kc, vc = k_ref[0, rows, :], v_ref[0, rows, :]
        gcol = gc_ref[0, rows, :]
        i_col = jnp.sum(jnp.where(lane == head * 4 + qi, gcol, 0.0), axis=1, keepdims=True)
        b_col = jnp.sum(jnp.where(lane == head * 4 + qb, gcol, 0.0), axis=1, keepdims=True)
        i_row, b_row = gr_ref[0, 0, qi:qi + 1, rows], gr_ref[0, 0, qb:qb + 1, rows]
        st_ref[c] = ct_ref[...].astype(BF16)
        nst_ref[d, c] = jnp.broadcast_to(nvec, nst_ref.shape[2:])
        mst_ref[c] = jnp.broadcast_to(m, mst_ref.shape[1:])
        bcol_ref[d, rows, :] = jnp.broadcast_to(b_col, (L, LANES))
        b_last = b_col[L - 1:L, :] if lower else b_col[0:1, :]
        g_col = b_last - b_col + i_col
        m_new = jnp.maximum(b_last + m, jnp.max(g_col, axis=0, keepdims=True))
        w_old = jnp.exp(b_last + m - m_new)
        w_s = jnp.exp(g_col - m_new)
        w_row = jnp.exp(b_last - b_row + i_row - m_new)
        wv = (w_s * vc.astype(F32)).astype(BF16)
        ct_ref[...] = w_old * ct_ref[...] + lax.dot_general(kc, wv, _TN, preferred_element_type=F32)
        dn = jnp.dot(jnp.broadcast_to(w_row, (8, L)).astype(BF16), kc, preferred_element_type=F32)
        return w_old * nvec + dn[0:1], m_new

    def scan_body(j, carry):
        nf, mf, nb, mb = carry
        nf, mf = state_step(j, 0, cf_ref, sf_ref, mf_ref, nf, mf, 0, 1, True)
        nb, mb = state_step(nc - 1 - j, 1, cb_ref, sb_ref, mb_ref, nb, mb, 2, 3, False)
        return nf, mf, nb, mb

    z_n = jnp.zeros((1, ML_DIM), F32)
    z_m = jnp.zeros((1, 1), F32)
    lax.fori_loop(0, nc, scan_body, (z_n, z_m, z_n, z_m), unroll=ML_UNROLL)

    ri = lax.broadcasted_iota(jnp.int32, (L, L), 0)
    ci = lax.broadcasted_iota(jnp.int32, (L, L), 1)

    def decay(b_rep, b_row, i_row, m, mask):
        dlog = jnp.where(mask, wide(b_rep, L // LANES) - b_row + i_row, -jnp.inf)
        inter = b_rep + m
        m_t = jnp.maximum(inter, jnp.max(dlog, axis=-1, keepdims=True))
        return jnp.exp(dlog - wide(m_t, L // LANES)), jnp.exp(inter - m_t), jnp.exp(-m_t)

    def out_body(c, carry):
        rows = pl.ds(pl.multiple_of(c * L, L), L)
        qc, kc, vc = q_ref[0, rows, :], k_ref[0, rows, :], v_ref[0, rows, :]
        grow = [gr_ref[0, 0, j:j + 1, rows] for j in range(4)]
        d_f, wi_f, e_f = decay(bcol_ref[0, rows, :], grow[1], grow[0], mf_ref[c][0:1, 0:1], ci <= ri)
        d_b, wi_b, e_b = decay(bcol_ref[1, rows, :], grow[3], grow[2], mb_ref[c][0:1, 0:1], ci >= ri)
        s = lax.dot_general(qc, kc, _NT, preferred_element_type=F32)
        s_f, s_b = s * d_f, s * d_b
        qn_f = lax.dot_general(qc, jnp.broadcast_to(nst_ref[0, c, 0:1, :].astype(BF16), (LANES, ML_DIM)), _NT,
                               preferred_element_type=F32)
        qn_b = lax.dot_general(qc, jnp.broadcast_to(nst_ref[1, c, 0:1, :].astype(BF16), (LANES, ML_DIM)), _NT,
                               preferred_element_type=F32)
        r_f = 1.0 / jnp.maximum(jnp.abs(wi_f * qn_f + jnp.sum(s_f, axis=-1, keepdims=True)), e_f)
        r_b = 1.0 / jnp.maximum(jnp.abs(wi_b * qn_b + jnp.sum(s_b, axis=-1, keepdims=True)), e_b)
        p = (s_f * wide(r_f, L // LANES) + s_b * wide(r_b, L // LANES)).astype(BF16)
        hs = (jnp.dot(qc, sf_ref[c], preferred_element_type=F32) * wide(wi_f * r_f, ML_DIM // LANES)
              + jnp.dot(qc, sb_ref[c], preferred_element_type=F32) * wide(wi_b * r_b, ML_DIM // LANES)
              + jnp.dot(p, vc, preferred_element_type=F32))
        mu = jnp.mean(hs, axis=-1, keepdims=True)
        var = jnp.mean(jnp.square(hs - mu), axis=-1, keepdims=True)
        o_ref[0, rows, :] = ((hs - mu) * lax.rsqrt(var + EPS) * gmh_ref[...]).astype(BF16)
        return carry

    lax.fori_loop(0, nc, out_body, 0, unroll=ML_UNROLL)


def _mlstm(q, k, v, gc, gr, gmh):
    B, S, _ = q.shape
    nc = S // ML_L
    tok = pl.BlockSpec((1, S, ML_DIM), lambda b, h: (b, 0, h))
    return pl.pallas_call(
        functools.partial(_mlstm_kernel, seq=S),
        grid=(B, ML_HEADS),
        in_specs=[tok, tok, tok,
                  pl.BlockSpec((1, S, LANES), lambda b, h: (b, 0, 0)),
                  pl.BlockSpec((1, 1, 4, S), lambda b, h: (b, h, 0, 0)),
                  pl.BlockSpec((1, ML_DIM), lambda b, h: (0, h))],
        out_specs=tok,
        out_shape=jax.ShapeDtypeStruct((B, S, ML_INNER), BF16),
        scratch_shapes=[pltpu.VMEM((ML_DIM, ML_DIM), F32), pltpu.VMEM((ML_DIM, ML_DIM), F32),
                        pltpu.VMEM((nc, ML_DIM, ML_DIM), BF16), pltpu.VMEM((nc, ML_DIM, ML_DIM), BF16),
                        pltpu.VMEM((2, nc, 8, ML_DIM), F32),
                        pltpu.VMEM((nc, 8, LANES), F32), pltpu.VMEM((nc, 8, LANES), F32),
                        pltpu.VMEM((2, S, LANES), F32)],
        compiler_params=_cparams(("parallel", "arbitrary")),
        name="mlstm",
    )(q, k, v, gc, gr, gmh)


def _out_kernel(hn_ref, xc_ref, z_ref, x1_ref, mod1_ref, skip_ref, wo_ref, gf_ref, y_ref):
    o = (hn_ref[0].astype(F32) + skip_ref[...] * xc_ref[0].astype(F32)) * _silu(z_ref[0].astype(F32))
    out = jnp.dot(o.astype(BF16), wo_ref[...], preferred_element_type=F32)
    x2 = x1_ref[0] + mod1_ref[0, 2:3, :] * out
    y_ref[0] = _rms(x2, gf_ref[...])


def _out(hn, xc, z, x1, mod1, skip, wo, gf, tm=512):
    B, S, D = x1.shape
    full = lambda a: pl.BlockSpec(a.shape, lambda b, i: (0,) * a.ndim)
    tok = lambda w: pl.BlockSpec((1, tm, w), lambda b, i: (b, i, 0))
    return pl.pallas_call(
        _out_kernel,
        grid=(B, S // tm),
        in_specs=[tok(ML_INNER), tok(ML_INNER), tok(ML_INNER), tok(D),
                  pl.BlockSpec((1, 3, D), lambda b, i: (b, 0, 0)), full(skip), full(wo), full(gf)],
        out_specs=tok(D),
        out_shape=jax.ShapeDtypeStruct((B, S, D), F32),
        compiler_params=_cparams(("parallel", "parallel")),
        name="out",
    )(hn, xc, z, x1, mod1, skip, wo, gf)


def _rope_tables(S):
    pos = jnp.arange(S, dtype=F32)
    inv = 1.0 / (ROPE_THETA ** (jnp.arange(0, QK_ROPE, 2, dtype=F32) / QK_ROPE))
    ang = pos[:, None] * inv[None, :]
    cos, sin = jnp.cos(ang), jnp.sin(ang)
    ones = jnp.ones((S, QK_NOPE), F32)
    z16 = jnp.zeros((S, ROPE_HALF), F32)
    z32 = jnp.zeros((S, HEAD_PAD - QK_NOPE - QK_ROPE), F32)
    z64 = jnp.zeros((S, QK_NOPE), F32)
    rc = jnp.concatenate([ones, cos, cos, z32], axis=-1)
    rm = jnp.concatenate([z64, -sin, z16, z32], axis=-1)
    rp = jnp.concatenate([z64, z16, sin, z32], axis=-1)
    return rc, rm, rp


def _na_bias(rpb):
    col = np.arange(GRID_W)
    col_start = np.clip(col - NA_WIN_W // 2, 0, GRID_W - NA_WIN_W)
    col_in = (col[None, :] >= col_start[:, None]) & (col[None, :] < col_start[:, None] + NA_WIN_W)
    period = 2 * GRID_W
    n_r = 2 * NA_WIN_H - 1
    vec = jnp.zeros((NA_HEADS, n_r, period), F32)
    vec = vec.at[..., :NA_WIN_W].set(rpb[..., NA_WIN_W - 1:])
    vec = vec.at[..., period - (NA_WIN_W - 1):].set(rpb[..., :NA_WIN_W - 1])
    toep = jnp.tile(vec, (1, 1, GRID_W))[..., :GRID_W * (period - 1)]
    toep = toep.reshape(NA_HEADS, n_r, GRID_W, period - 1)[..., :GRID_W]
    b = jnp.stack([toep[:, NA_WIN_H - 1 - d:2 * NA_WIN_H - 1 - d] for d in range(NA_WIN_H)], axis=1)
    b = jnp.where(col_in[None, None, None], b, NEG_BIG)
    b = b.transpose(0, 1, 3, 2, 4).reshape(NA_HEADS, NA_WIN_H, GRID_W, NA_KEYS)
    b = b.reshape(NA_HEADS // NA_GROUP, NA_GROUP, NA_WIN_H, GRID_W, NA_KEYS).transpose(0, 2, 1, 3, 4)
    return b.reshape(NA_HEADS // NA_GROUP, NA_WIN_H, NA_LANES, NA_KEYS).astype(F32)


def _block_diag(w):
    per = BD_TILE // QKV_BLOCK
    wt = w.transpose(0, 2, 1).reshape(N_BD, per, QKV_BLOCK, QKV_BLOCK)
    eye = jnp.eye(per, dtype=w.dtype)
    return jnp.einsum('caio,ab->caibo', wt, eye).reshape(N_BD, BD_TILE, BD_TILE)


def _tri(tm, lower):
    t = np.arange(tm)
    same = (t[:, None] // ML_L) == (t[None, :] // ML_L)
    tri = (t[None, :] <= t[:, None]) if lower else (t[None, :] >= t[:, None])
    return jnp.asarray(same & tri, dtype=BF16)


def _prep(g_norm, g_final, w_in0, g_qlat, g_kvlat, w_uq, w_ukv, na_rpb, w_out0,
          w_in1, conv_w, conv_b, w_q, w_k, w_v, w_gate, b_gate, g_mh, skip, w_out1):
    p = {}
    w0 = w_in0[0]
    o = np.cumsum((Q_LORA, KV_LORA, QK_ROPE, NA_WIDTH, NA_WIDTH, NA_WIDTH))
    kr = jnp.zeros((D_MODEL, HEAD_PAD), F32).at[:, QK_NOPE:QK_NOPE + QK_ROPE].set(w0[:, o[1]:o[2]])
    p['wa'] = jnp.concatenate([w0[:, :o[1]], kr], axis=-1).astype(BF16)
    p['wb'] = w0[:, o[2]:].astype(BF16)
    wq = w_uq[0].reshape(Q_LORA, MLA_HEADS, QK_NOPE + QK_ROPE)
    p['wuq'] = jnp.pad(wq, ((0, 0), (0, 0), (0, HEAD_PAD - QK_NOPE - QK_ROPE))).reshape(Q_LORA, MLA_QK).astype(BF16)
    wkv = w_ukv[0].reshape(KV_LORA, MLA_HEADS, QK_NOPE + V_DIM)
    wk = jnp.pad(wkv[:, :, :QK_NOPE], ((0, 0), (0, 0), (0, HEAD_PAD - QK_NOPE))).reshape(KV_LORA, MLA_QK)
    wv = wkv[:, :, QK_NOPE:].reshape(KV_LORA, MLA_WIDTH)
    p['wukv'] = jnp.concatenate([wk, wv], axis=-1).astype(BF16)
    p['gq'] = g_qlat[0].reshape(1, Q_LORA)
    p['gkv'] = g_kvlat[0].reshape(1, KV_LORA)
    p['g0'] = g_norm[0].reshape(1, D_MODEL)
    p['g1'] = g_norm[1].reshape(1, D_MODEL)
    p['gf'] = g_final.reshape(1, D_MODEL)
    p['bias'] = _na_bias(na_rpb[0])
    p['wo0'] = w_out0[0].astype(BF16)
    p['wi1'] = w_in1[0].astype(BF16)
    p['cw'] = conv_w[0]
    p['cb'] = conv_b[0].reshape(1, ML_INNER)
    p['bdqk'] = jnp.concatenate([_block_diag(w_q[0]), _block_diag(w_k[0])], axis=-1).astype(BF16)
    p['bdv'] = _block_diag(w_v[0]).astype(BF16)
    wg = w_gate[0].reshape(3, N_BD, BD_TILE, 4, ML_HEADS).transpose(1, 0, 2, 4, 3)
    wg = wg.reshape(N_BD, 3 * BD_TILE, 4 * ML_HEADS)
    p['wg'] = jnp.pad(wg, ((0, 0), (0, 0), (0, LANES - 4 * ML_HEADS))).astype(BF16)
    bg = b_gate[0].reshape(4, ML_HEADS).T.reshape(1, 4 * ML_HEADS)
    p['bg'] = jnp.pad(bg, ((0, 0), (0, LANES - 4 * ML_HEADS)))
    p['gmh'] = g_mh[0].reshape(1, ML_INNER)
    p['skip'] = skip[0].reshape(1, ML_INNER)
    p['wo1'] = w_out1[0].astype(BF16)
    return p


def _trunk(x, mod0, mod1, p, tabs, tm=512):
    B, S, _ = x.shape
    rc, rm, rp = tabs
    q, k, v, nq, nk, nv, gate = _in0(x, mod0, p['g0'], p['wa'], p['wb'], p['gq'], p['gkv'],
                                      p['wuq'], p['wukv'], rc, rm, rp, tm=tm)
    o_mla = _mla(q, k, v)
    o_na = _natten(nq, nk, nv, p['bias'])
    x1, xm, z = _mid(o_mla, o_na, gate, x, mod0, mod1, p['g1'], p['wo0'], p['wi1'], tm=tm)
    t1, t2 = _tri(tm, True), _tri(tm, False)
    mq, mk, mv, xc, gc, gr = _conv(xm, p['cw'], p['cb'], p['bdqk'], p['bdv'], p['wg'], p['bg'], t1, t2, tm=tm)
    hn = _mlstm(mq, mk, mv, gc, gr.reshape(B, ML_HEADS, 4, S), p['gmh'])
    return _out(hn, xc, z, x1, mod1, p['skip'], p['wo1'], p['gf'], tm=tm)


def kernel(x_prompt, x_sample, c_prompt, c_sample, g_norm, w_ada, b_ada, g_final, w_in0, g_qlat, g_kvlat, w_uq, w_ukv, na_rpb, w_out0, w_in1, conv_w, conv_b, w_q, w_k, w_v, w_gate, b_gate, g_mh, skip, w_out1):
    p = _prep(g_norm, g_final, w_in0, g_qlat, g_kvlat, w_uq, w_ukv, na_rpb, w_out0,
              w_in1, conv_w, conv_b, w_q, w_k, w_v, w_gate, b_gate, g_mh, skip, w_out1)
    nb_p = x_prompt.shape[0]
    mod = _ada(jnp.concatenate([c_prompt, c_sample], axis=0), w_ada, b_ada)
    mod = mod.reshape(DEPTH, -1, 3, D_MODEL)
    outs = []
    for x, sl in ((x_prompt, slice(0, nb_p)), (x_sample, slice(nb_p, None))):
        tabs = _rope_tables(x.shape[1])
        outs.append(_trunk(x, mod[0, sl], mod[1, sl], p, tabs))
    return tuple(outs)
```

```python
import functools

import numpy as np
import jax
import jax.numpy as jnp
from jax import lax
from jax.experimental import pallas as pl
from jax.experimental.pallas import tpu as pltpu

F32 = jnp.float32
BF16 = jnp.bfloat16

D_MODEL = 1024
DEPTH = 2
GRID_W = 64
EPS = 1e-6
MLA_HEADS = 8
QK_NOPE = 64
QK_ROPE = 32
V_DIM = 64
Q_LORA = 256
KV_LORA = 256
ROPE_THETA = 10000.0
MLA_WIDTH = MLA_HEADS * V_DIM
NA_HEADS = 8
NA_DIM = 64
NA_WIN_H = 8
NA_WIN_W = 16
NA_WIDTH = NA_HEADS * NA_DIM
ML_HEADS = 4
ML_INNER = 2 * D_MODEL
ML_DIM = ML_INNER // ML_HEADS
ML_CONV = 5
QKV_BLOCK = 4
ML_L = 256

HEAD_PAD = 128
MLA_QK = MLA_HEADS * HEAD_PAD
ROPE_HALF = QK_ROPE // 2
LANES = 128
BD_TILE = 256
N_BD = ML_INNER // BD_TILE
NEG_BIG = -1e30
LOG2E = 1.4426950408889634

VMEM_LIMIT = 56 * 1024 * 1024

_NT = (((1,), (1,)), ((), ()))
_TN = (((0,), (0,)), ((), ()))


def _cparams(sem):
    return pltpu.CompilerParams(dimension_semantics=sem, vmem_limit_bytes=VMEM_LIMIT)


def _silu(x):
    return x * jax.nn.sigmoid(x)


def _rms(x, g):
    return x * lax.rsqrt(jnp.mean(x * x, axis=-1, keepdims=True) + EPS) * g


def _ada_kernel(c_ref, w_ref, b_ref, o_ref):
    c = c_ref[...]
    o_ref[0] = jnp.dot(_silu(c).astype(BF16), w_ref[0].astype(BF16), preferred_element_type=F32) + b_ref[0]


def _ada(c_all, w_ada, b_ada):
    nb = c_all.shape[0]
    tn = 512
    return pl.pallas_call(
        _ada_kernel,
        grid=(DEPTH, 3 * D_MODEL // tn),
        in_specs=[pl.BlockSpec((nb, D_MODEL), lambda l, j: (0, 0)),
                  pl.BlockSpec((1, D_MODEL, tn), lambda l, j: (l, 0, j)),
                  pl.BlockSpec((1, 1, tn), lambda l, j: (l, 0, j))],
        out_specs=pl.BlockSpec((1, nb, tn), lambda l, j: (l, 0, j)),
        out_shape=jax.ShapeDtypeStruct((DEPTH, nb, 3 * D_MODEL), F32),
        compiler_params=_cparams(("arbitrary", "arbitrary")),
        name="ada",
    )(c_all, w_ada, b_ada.reshape(DEPTH, 1, 3 * D_MODEL))


def _in0_kernel(x_ref, mod_ref, g_ref, wa_ref, wb_ref, gq_ref, gkv_ref, wuq_ref, wukv_ref,
                rc_ref, rm_ref, rp_ref,
                q_ref, k_ref, v_ref, nq_ref, nk_ref, nv_ref, gate_ref):
    x = x_ref[0]
    shift = mod_ref[0, 0:1, :]
    scale = mod_ref[0, 1:2, :]
    h = _rms(x, g_ref[...]) * (1.0 + scale) + shift
    hb = h.astype(BF16)

    ua = jnp.dot(hb, wa_ref[...], preferred_element_type=F32)
    qn = _rms(ua[:, :Q_LORA], gq_ref[...]).astype(BF16)
    kvn = _rms(ua[:, Q_LORA:Q_LORA + KV_LORA], gkv_ref[...]).astype(BF16)
    kr = ua[:, Q_LORA + KV_LORA:]
    qf = jnp.dot(qn, wuq_ref[...], preferred_element_type=F32)
    kvf = jnp.dot(kvn, wukv_ref[...], preferred_element_type=F32)

    rc, rm, rp = rc_ref[...], rm_ref[...], rp_ref[...]

    def rope(t):
        return (t * rc + pltpu.roll(t, LANES - ROPE_HALF, 1) * rm + pltpu.roll(t, ROPE_HALF, 1) * rp)

    krr = rope(kr)
    qscale = (QK_NOPE + QK_ROPE) ** -0.5 * LOG2E
    for hh in range(MLA_HEADS):
        sl = slice(hh * HEAD_PAD, (hh + 1) * HEAD_PAD)
        q_ref[0, :, sl] = (rope(qf[:, sl]) * qscale).astype(BF16)
        k_ref[0, :, sl] = (kvf[:, sl] + krr).astype(BF16)
    ones_pad = (lax.broadcasted_iota(jnp.int32, (1, MLA_QK), 1) % HEAD_PAD >= V_DIM).astype(F32)
    v_ref[0] = (kvf[:, MLA_QK:] + ones_pad).astype(BF16)

    ub = jnp.dot(hb, wb_ref[...], preferred_element_type=F32)
    nq_ref[0] = (ub[:, :NA_WIDTH] * (NA_DIM ** -0.5)).astype(BF16)
    nk_ref[0] = ub[:, NA_WIDTH:2 * NA_WIDTH].astype(BF16)
    nv_ref[0] = ub[:, 2 * NA_WIDTH:3 * NA_WIDTH].astype(BF16)
    gate_ref[0] = ub[:, 3 * NA_WIDTH:].astype(BF16)


def _in0(x, mod, g, wa, wb, gq, gkv, wuq, wukv, rc, rm, rp, tm=512):
    B, S, D = x.shape
    full = lambda a: pl.BlockSpec(a.shape, lambda b, i: (0,) * a.ndim)
    tok = lambda w: pl.BlockSpec((1, tm, w), lambda b, i: (b, i, 0))
    tab = pl.BlockSpec((tm, LANES), lambda b, i: (i, 0))
    widths = (MLA_QK, MLA_QK, MLA_QK, NA_WIDTH, NA_WIDTH, NA_WIDTH, MLA_WIDTH + NA_WIDTH)
    return pl.pallas_call(
        _in0_kernel,
        grid=(B, S // tm),
        in_specs=[tok(D), pl.BlockSpec((1, 3, D), lambda b, i: (b, 0, 0)), full(g), full(wa), full(wb),
                  full(gq), full(gkv), full(wuq), full(wukv), tab, tab, tab],
        out_specs=[tok(w) for w in widths],
        out_shape=[jax.ShapeDtypeStruct((B, S, w), BF16) for w in widths],
        compiler_params=_cparams(("parallel", "parallel")),
        name="in0",
    )(x, mod, g, wa, wb, gq, gkv, wuq, wukv, rc, rm, rp)


def _mla_kernel(q_ref, k_ref, v_ref, o_ref):
    low = lax.broadcasted_iota(jnp.int32, (q_ref.shape[1], HEAD_PAD), 1) < V_DIM
    for pair in range(MLA_HEADS // 2):
        outs = []
        for j in range(2):
            hh = 2 * pair + j
            sl = slice(hh * HEAD_PAD, (hh + 1) * HEAD_PAD)
            s = lax.dot_general(q_ref[0, :, sl], k_ref[0, :, sl], _NT, preferred_element_type=F32)
            p = jnp.exp2(s - jnp.max(s, axis=-1, keepdims=True))
            ol = jnp.dot(p.astype(BF16), v_ref[0, :, sl], preferred_element_type=F32)
            outs.append(ol / jnp.where(low, pltpu.roll(ol, V_DIM, 1), 1.0))
        vs = slice(pair * 2 * V_DIM, (pair + 1) * 2 * V_DIM)
        o_ref[0, :, vs] = jnp.where(low, outs[0], pltpu.roll(outs[1], V_DIM, 1)).astype(BF16)


def _mla(q, k, v, tq=512):
    B, S, _ = q.shape
    return pl.pallas_call(
        _mla_kernel,
        grid=(B, S // tq),
        in_specs=[pl.BlockSpec((1, tq, MLA_QK), lambda b, i: (b, i, 0)),
                  pl.BlockSpec((1, S, MLA_QK), lambda b, i: (b, 0, 0)),
                  pl.BlockSpec((1, S, MLA_QK), lambda b, i: (b, 0, 0))],
        out_specs=pl.BlockSpec((1, tq, MLA_WIDTH), lambda b, i: (b, i, 0)),
        out_shape=jax.ShapeDtypeStruct((B, S, MLA_WIDTH), BF16),
        compiler_params=_cparams(("parallel", "arbitrary")),
        name="mla",
    )(q, k, v)


NA_GROUP = 4
NA_LANES = NA_GROUP * NA_DIM
NA_KEYS = NA_WIN_H * GRID_W
NA_UNROLL = 4


def _na_kernel(q_ref, k_ref, v_ref, bias_ref, o_ref, *, rows):
    rh = lax.broadcasted_iota(jnp.int32, (NA_LANES, NA_LANES), 0) // GRID_W
    lh = lax.broadcasted_iota(jnp.int32, (NA_LANES, NA_LANES), 1) // NA_DIM
    hmask = (rh == lh).astype(F32)

    def body(r, carry):
        r0 = jnp.clip(r - NA_WIN_H // 2, 0, rows - NA_WIN_H)
        d = r - r0
        qr = q_ref[0, pl.ds(pl.multiple_of(r * GRID_W, GRID_W), GRID_W), :].astype(F32)
        qs = (jnp.concatenate([qr] * NA_GROUP, axis=0) * hmask).astype(BF16)
        win = pl.ds(pl.multiple_of(r0 * GRID_W, GRID_W), NA_KEYS)
        s = lax.dot_general(qs, k_ref[0, win, :], _NT, preferred_element_type=F32) + bias_ref[0, d]
        m = jnp.max(s, axis=-1, keepdims=True)
        p = jnp.exp(s - m)
        l = jnp.sum(p, axis=-1, keepdims=True)
        o = jnp.dot(p.astype(BF16), v_ref[0, win, :], preferred_element_type=F32) / l * hmask
        out = o[0:GRID_W]
        for g in range(1, NA_GROUP):
            out = out + o[g * GRID_W:(g + 1) * GRID_W]
        o_ref[0, pl.ds(pl.multiple_of(r * GRID_W, GRID_W), GRID_W), :] = out.astype(BF16)
        return carry

    lax.fori_loop(0, rows, body, 0, unroll=NA_UNROLL)


def _natten(q, k, v, bias):
    B, S, _ = q.shape
    rows = S // GRID_W
    tok = pl.BlockSpec((1, S, NA_LANES), lambda g, b: (b, 0, g))
    return pl.pallas_call(
        functools.partial(_na_kernel, rows=rows),
        grid=(NA_HEADS // NA_GROUP, B),
        in_specs=[tok, tok, tok,
                  pl.BlockSpec((1, NA_WIN_H, NA_LANES, NA_KEYS), lambda g, b: (g, 0, 0, 0))],
        out_specs=tok,
        out_shape=jax.ShapeDtypeStruct((B, S, NA_WIDTH), BF16),
        compiler_params=_cparams(("parallel", "parallel")),
        name="natten",
    )(q, k, v, bias)


def _mid_kernel(om_ref, on_ref, gate_ref, x_ref, mod0_ref, mod1_ref, g_ref, wo_ref, wi_ref,
                x1_ref, xm_ref, z_ref):
    sg = _silu(gate_ref[0].astype(F32))
    o = jnp.concatenate([om_ref[0].astype(F32), on_ref[0].astype(F32)], axis=-1) * sg
    out = jnp.dot(o.astype(BF16), wo_ref[...], preferred_element_type=F32)
    x1 = x_ref[0] + mod0_ref[0, 2:3, :] * out
    x1_ref[0] = x1
    h = _rms(x1, g_ref[...]) * (1.0 + mod1_ref[0, 1:2, :]) + mod1_ref[0, 0:1, :]
    u = jnp.dot(h.astype(BF16), wi_ref[...], preferred_element_type=F32)
    xm_ref[0] = u[:, :ML_INNER].astype(BF16)
    z_ref[0] = u[:, ML_INNER:].astype(BF16)


def _mid(om, on, gate, x, mod0, mod1, g, wo, wi, tm=512):
    B, S, D = x.shape
    full = lambda a: pl.BlockSpec(a.shape, lambda b, i: (0,) * a.ndim)
    tok = lambda w: pl.BlockSpec((1, tm, w), lambda b, i: (b, i, 0))
    modspec = pl.BlockSpec((1, 3, D), lambda b, i: (b, 0, 0))
    return pl.pallas_call(
        _mid_kernel,
        grid=(B, S // tm),
        in_specs=[tok(MLA_WIDTH), tok(NA_WIDTH), tok(D), tok(D), modspec, modspec, full(g), full(wo), full(wi)],
        out_specs=[tok(D), tok(ML_INNER), tok(ML_INNER)],
        out_shape=[jax.ShapeDtypeStruct((B, S, D), F32),
                   jax.ShapeDtypeStruct((B, S, ML_INNER), BF16),
                   jax.ShapeDtypeStruct((B, S, ML_INNER), BF16)],
        compiler_params=_cparams(("parallel", "parallel")),
        name="mid",
    )(om, on, gate, x, mod0, mod1, g, wo, wi)


HALO = 16
SHIFT_ROWS = 128
SHIFT_WIN = SHIFT_ROWS + 2 * HALO
GATE_ROWS = 2 * ML_HEADS
PADR = 8
MXU_TAPS = (ML_CONV // 2 - 1, ML_CONV // 2 + 1)
VALU_TAPS = tuple(j for j in range(ML_CONV) if j not in MXU_TAPS)


def _log_sigmoid(x):
    return jnp.minimum(x, 0.0) - jnp.log1p(jnp.exp(-jnp.abs(x)))


def _chunk_scan(x, pos, op, ident, reverse):
    n = x.shape[0]
    k = 1
    while k < ML_L:
        if reverse:
            x = op(x, jnp.where(pos < ML_L - k, pltpu.roll(x, n - k, 0), ident))
        else:
            x = op(x, jnp.where(pos >= k, pltpu.roll(x, k, 0), ident))
        k *= 2
    return x


def _conv_kernel(xm_ref, xp_ref, xn_ref, sh_ref, cw_ref, cb_ref, bdqk_ref, bdv_ref, bg_ref,
                 q_ref, k_ref, v_ref, xc_ref, pre_ref, ext_ref, *, tm):
    i = pl.program_id(1)
    n = pl.num_programs(1)
    ext_ref[0:PADR, :] = jnp.where(i > 0, xp_ref[0].astype(F32)[HALO - PADR:, :], 0.0)
    ext_ref[PADR:PADR + tm, :] = xm_ref[0].astype(F32)
    ext_ref[PADR + tm:, :] = jnp.where(i < n - 1, xn_ref[0].astype(F32)[:PADR, :], 0.0)
    zero_halo = jnp.zeros((HALO, BD_TILE), BF16)
    qscale = ML_DIM ** -0.5
    pre = jnp.zeros((tm, LANES), F32) + bg_ref[...]
    for c in range(N_BD):
        cs = slice(c * BD_TILE, (c + 1) * BD_TILE)
        prev = jnp.where(i > 0, xp_ref[0, :, cs], zero_halo)
        nxt = jnp.where(i < n - 1, xn_ref[0, :, cs], zero_halo)
        ys = []
        for rb in range(tm // SHIFT_ROWS):
            lo, hi = rb * SHIFT_ROWS - HALO, (rb + 1) * SHIFT_ROWS + HALO
            parts = [prev] if lo < 0 else []
            parts.append(xm_ref[0, max(lo, 0):min(hi, tm), cs])
            if hi > tm:
                parts.append(nxt)
            win = jnp.concatenate(parts, axis=0) if len(parts) > 1 else parts[0]
            sh = jnp.dot(sh_ref[...], win, preferred_element_type=F32)
            y = cb_ref[:, cs]
            for t, j in enumerate(MXU_TAPS):
                y = y + sh[t * SHIFT_ROWS:(t + 1) * SHIFT_ROWS] * cw_ref[j:j + 1, cs]
            for j in VALU_TAPS:
                r0 = PADR + rb * SHIFT_ROWS + j - ML_CONV // 2
                y = y + ext_ref[r0:r0 + SHIFT_ROWS, cs] * cw_ref[j:j + 1, cs]
            ys.append(y)
        xc = _silu(jnp.concatenate(ys, axis=0))
        xcb = xc.astype(BF16)
        xc_ref[0, :, cs] = xcb
        qk = jnp.dot(xcb, bdqk_ref[c], preferred_element_type=F32)
        vv = jnp.dot(xm_ref[0, :, cs], bdv_ref[c], preferred_element_type=F32)
        pre = pre + qk[:, 2 * BD_TILE:] + vv[:, BD_TILE:]
        q_ref[0, :, cs] = (qk[:, :BD_TILE] * qscale).astype(BF16)
        k_ref[0, :, cs] = qk[:, BD_TILE:2 * BD_TILE].astype(BF16)
        v_ref[0, :, cs] = vv[:, :BD_TILE].astype(BF16)

    pre_ref[0] = pre


def _gates_kernel(pre_ref, gc_ref, gr_ref):
    pre = pre_ref[0]
    lane = lax.broadcasted_iota(jnp.int32, pre.shape, 1)
    pos = lax.broadcasted_iota(jnp.int32, pre.shape, 0) % ML_L
    fwd = lane % 2 == 0
    lf = _log_sigmoid(pre)
    b = jnp.where(fwd, _chunk_scan(lf, pos, jnp.add, 0.0, False), _chunk_scan(lf, pos, jnp.add, 0.0, True))
    a = pre - pltpu.roll(b, LANES - GATE_ROWS, 1)
    c = jnp.where(fwd, _chunk_scan(a, pos, jnp.maximum, -jnp.inf, False),
                  _chunk_scan(a, pos, jnp.maximum, -jnp.inf, True))
    g = jnp.where(lane < GATE_ROWS, a, jnp.where(lane < 2 * GATE_ROWS, b, pltpu.roll(c, 2 * GATE_ROWS, 1)))
    gc_ref[0] = g
    gr_ref[0] = g.T[:3 * GATE_ROWS, :]


def _shift_matrix():
    m = np.zeros((len(MXU_TAPS) * SHIFT_ROWS, SHIFT_WIN), np.float32)
    r = np.arange(SHIFT_ROWS)
    for t, j in enumerate(MXU_TAPS):
        m[t * SHIFT_ROWS + r, r + HALO + j - ML_CONV // 2] = 1.0
    return jnp.asarray(m, dtype=BF16)


def _conv(xm, cw, cb, bdqk, bdv, bg, tm=512):
    sh = _shift_matrix()
    B, S, _ = xm.shape
    hb = tm // HALO
    full = lambda a: pl.BlockSpec(a.shape, lambda b, i: (0,) * a.ndim)
    tok = lambda w: pl.BlockSpec((1, tm, w), lambda b, i: (b, i, 0))
    return pl.pallas_call(
        functools.partial(_conv_kernel, tm=tm),
        grid=(B, S // tm),
        in_specs=[tok(ML_INNER),
                  pl.BlockSpec((1, HALO, ML_INNER), lambda b, i: (b, jnp.maximum(i * hb - 1, 0), 0)),
                  pl.BlockSpec((1, HALO, ML_INNER), lambda b, i: (b, jnp.minimum((i + 1) * hb, S // HALO - 1), 0)),
                  full(sh), full(cw), full(cb), full(bdqk), full(bdv), full(bg)],
        out_specs=[tok(ML_INNER), tok(ML_INNER), tok(ML_INNER), tok(ML_INNER), tok(LANES)],
        out_shape=[jax.ShapeDtypeStruct((B, S, ML_INNER), BF16)] * 4 + [jax.ShapeDtypeStruct((B, S, LANES), F32)],
        scratch_shapes=[pltpu.VMEM((tm + 2 * PADR, ML_INNER), F32)],
        compiler_params=_cparams(("parallel", "parallel")),
        name="conv",
    )(xm, xm, xm, sh, cw, cb, bdqk, bdv, bg)


def _gates(pre):
    B, S, _ = pre.shape
    return pl.pallas_call(
        _gates_kernel,
        grid=(B,),
        in_specs=[pl.BlockSpec((1, S, LANES), lambda b: (b, 0, 0))],
        out_specs=[pl.BlockSpec((1, S, LANES), lambda b: (b, 0, 0)),
                   pl.BlockSpec((1, 3 * GATE_ROWS, S), lambda b: (b, 0, 0))],
        out_shape=[jax.ShapeDtypeStruct((B, S, LANES), F32), jax.ShapeDtypeStruct((B, 3 * GATE_ROWS, S), F32)],
        compiler_params=_cparams(("parallel",)),
        name="gates",
    )(pre)


ML_UNROLL = 2


def _mlstm_kernel(q_ref, k_ref, v_ref, gc_ref, gr_ref, gmh_ref, o_ref,
                  cf_ref, cb_ref, sf_ref, sb_ref, nst_ref, mf_ref, mb_ref, mrep_ref, brep_ref, *, seq):
    L = ML_L
    nc = seq // L
    head = pl.program_id(1)
    cf_ref[...] = jnp.zeros_like(cf_ref)
    cb_ref[...] = jnp.zeros_like(cb_ref)
    lane = lax.broadcasted_iota(jnp.int32, (L, LANES), 1)

    def wide(x, n):
        return jnp.concatenate([x] * n, axis=1)

    def state_step(c, d, ct_ref, st_ref, mst_ref, nvec, m):
        rows = pl.ds(pl.multiple_of(c * L, L), L)
        r = head * 2 + d
        kc, vc = k_ref[0, rows, :], v_ref[0, rows, :]
        gcol = gc_ref[0, rows, :]
        a_col, b_col, c_col = [jnp.sum(jnp.where(lane == g * GATE_ROWS + r, gcol, 0.0), axis=1, keepdims=True)
                               for g in range(3)]
        a_row = gr_ref[0, pl.ds(r, 1), rows]
        st_ref[c] = ct_ref[...].astype(BF16)
        nst_ref[d, c] = jnp.broadcast_to(nvec, nst_ref.shape[2:])
        mst_ref[c] = jnp.broadcast_to(m, mst_ref.shape[1:])
        mrep = jnp.maximum(c_col, m)
        mrep_ref[d, rows, :] = jnp.broadcast_to(mrep, (L, LANES))
        brep_ref[d, rows, :] = jnp.broadcast_to(b_col + mrep, (L, LANES))
        end = L - 1 if d == 0 else 0
        b_last, c_last = b_col[end:end + 1, :], c_col[end:end + 1, :]
        m_new = b_last + jnp.maximum(m, c_last)
        w_old = jnp.exp(b_last + m - m_new)
        w_s = jnp.exp(b_last + a_col - m_new)
        w_row = jnp.exp(b_last + a_row - m_new)
        wv = (w_s * vc.astype(F32)).astype(BF16)
        ct_ref[...] = w_old * ct_ref[...] + lax.dot_general(kc, wv, _TN, preferred_element_type=F32)
        dn = jnp.dot(jnp.broadcast_to(w_row, (8, L)).astype(BF16), kc, preferred_element_type=F32)
        return w_old * nvec + dn[0:1], m_new

    def scan_body(j, carry):
        nf, mf, nb, mb = carry
        nf, mf = state_step(j, 0, cf_ref, sf_ref, mf_ref, nf, mf)
        nb, mb = state_step(nc - 1 - j, 1, cb_ref, sb_ref, mb_ref, nb, mb)
        return nf, mf, nb, mb

    z_n = jnp.zeros((1, ML_DIM), F32)
    z_m = jnp.zeros((1, 1), F32)
    lax.fori_loop(0, nc, scan_body, (z_n, z_m, z_n, z_m), unroll=ML_UNROLL)

    ri = lax.broadcasted_iota(jnp.int32, (L, L), 0)
    ci = lax.broadcasted_iota(jnp.int32, (L, L), 1)

    def decay(d, rows, m, mask):
        a_row = gr_ref[0, pl.ds(head * 2 + d, 1), rows]
        mrep = mrep_ref[d, rows, :]
        dmat = jnp.exp(jnp.where(mask, a_row - wide(mrep, L // LANES), -jnp.inf))
        return dmat, jnp.exp(m - mrep), jnp.exp(-brep_ref[d, rows, :])

    def out_body(c, carry):
        rows = pl.ds(pl.multiple_of(c * L, L), L)
        qc, kc, vc = q_ref[0, rows, :], k_ref[0, rows, :], v_ref[0, rows, :]
        d_f, wi_f, e_f = decay(0, rows, mf_ref[c][0:1, 0:1], ci <= ri)
        d_b, wi_b, e_b = decay(1, rows, mb_ref[c][0:1, 0:1], ci >= ri)
        s = lax.dot_general(qc, kc, _NT, preferred_element_type=F32)
        s_f, s_b = s * d_f, s * d_b
        qn_f = lax.dot_general(qc, jnp.broadcast_to(nst_ref[0, c, 0:1, :].astype(BF16), (LANES, ML_DIM)), _NT,
                               preferred_element_type=F32)
        qn_b = lax.dot_general(qc, jnp.broadcast_to(nst_ref[1, c, 0:1, :].astype(BF16), (LANES, ML_DIM)), _NT,
                               preferred_element_type=F32)
        r_f = 1.0 / jnp.maximum(jnp.abs(wi_f * qn_f + jnp.sum(s_f, axis=-1, keepdims=True)), e_f)
        r_b = 1.0 / jnp.maximum(jnp.abs(wi_b * qn_b + jnp.sum(s_b, axis=-1, keepdims=True)), e_b)
        p = (s_f * wide(r_f, L // LANES) + s_b * wide(r_b, L // LANES)).astype(BF16)
        hs = (jnp.dot(qc, sf_ref[c], preferred_element_type=F32) * wide(wi_f * r_f, ML_DIM // LANES)
              + jnp.dot(qc, sb_ref[c], preferred_element_type=F32) * wide(wi_b * r_b, ML_DIM // LANES)
              + jnp.dot(p, vc, preferred_element_type=F32))
        mu = jnp.mean(hs, axis=-1, keepdims=True)
        var = jnp.mean(jnp.square(hs - mu), axis=-1, keepdims=True)
        o_ref[0, rows, :] = ((hs - mu) * lax.rsqrt(var + EPS) * gmh_ref[...]).astype(BF16)
        return carry

    lax.fori_loop(0, nc, out_body, 0, unroll=2 * ML_UNROLL)


def _mlstm(q, k, v, gc, gr, gmh):
    B, S, _ = q.shape
    nc = S // ML_L
    tok = pl.BlockSpec((1, S, ML_DIM), lambda b, h: (b, 0, h))
    return pl.pallas_call(
        functools.partial(_mlstm_kernel, seq=S),
        grid=(B, ML_HEADS),
        in_specs=[tok, tok, tok,
                  pl.BlockSpec((1, S, LANES), lambda b, h: (b, 0, 0)),
                  pl.BlockSpec((1, 3 * GATE_ROWS, S), lambda b, h: (b, 0, 0)),
                  pl.BlockSpec((1, ML_DIM), lambda b, h: (0, h))],
        out_specs=tok,
        out_shape=jax.ShapeDtypeStruct((B, S, ML_INNER), BF16),
        scratch_shapes=[pltpu.VMEM((ML_DIM, ML_DIM), F32), pltpu.VMEM((ML_DIM, ML_DIM), F32),
                        pltpu.VMEM((nc, ML_DIM, ML_DIM), BF16), pltpu.VMEM((nc, ML_DIM, ML_DIM), BF16),
                        pltpu.VMEM((2, nc, 8, ML_DIM), F32),
                        pltpu.VMEM((nc, 8, LANES), F32), pltpu.VMEM((nc, 8, LANES), F32),
                        pltpu.VMEM((2, S, LANES), F32), pltpu.VMEM((2, S, LANES), F32)],
        compiler_params=_cparams(("parallel", "arbitrary")),
        name="mlstm",
    )(q, k, v, gc, gr, gmh)


def _out_kernel(hn_ref, xc_ref, z_ref, x1_ref, mod1_ref, skip_ref, wo_ref, gf_ref, y_ref):
    o = (hn_ref[0].astype(F32) + skip_ref[...] * xc_ref[0].astype(F32)) * _silu(z_ref[0].astype(F32))
    out = jnp.dot(o.astype(BF16), wo_ref[...], preferred_element_type=F32)
    x2 = x1_ref[0] + mod1_ref[0, 2:3, :] * out
    y_ref[0] = _rms(x2, gf_ref[...])


def _out(hn, xc, z, x1, mod1, skip, wo, gf, tm=512):
    B, S, D = x1.shape
    full = lambda a: pl.BlockSpec(a.shape, lambda b, i: (0,) * a.ndim)
    tok = lambda w: pl.BlockSpec((1, tm, w), lambda b, i: (b, i, 0))
    return pl.pallas_call(
        _out_kernel,
        grid=(B, S // tm),
        in_specs=[tok(ML_INNER), tok(ML_INNER), tok(ML_INNER), tok(D),
                  pl.BlockSpec((1, 3, D), lambda b, i: (b, 0, 0)), full(skip), full(wo), full(gf)],
        out_specs=tok(D),
        out_shape=jax.ShapeDtypeStruct((B, S, D), F32),
        compiler_params=_cparams(("parallel", "parallel")),
        name="out",
    )(hn, xc, z, x1, mod1, skip, wo, gf)


def _rope_tables(S):
    pos = jnp.arange(S, dtype=F32)
    inv = 1.0 / (ROPE_THETA ** (jnp.arange(0, QK_ROPE, 2, dtype=F32) / QK_ROPE))
    ang = pos[:, None] * inv[None, :]
    cos, sin = jnp.cos(ang), jnp.sin(ang)
    ones = jnp.ones((S, QK_NOPE), F32)
    z16 = jnp.zeros((S, ROPE_HALF), F32)
    z32 = jnp.zeros((S, HEAD_PAD - QK_NOPE - QK_ROPE), F32)
    z64 = jnp.zeros((S, QK_NOPE), F32)
    rc = jnp.concatenate([ones, cos, cos, z32], axis=-1)
    rm = jnp.concatenate([z64, -sin, z16, z32], axis=-1)
    rp = jnp.concatenate([z64, z16, sin, z32], axis=-1)
    return rc, rm, rp


def _na_bias(rpb):
    col = np.arange(GRID_W)
    col_start = np.clip(col - NA_WIN_W // 2, 0, GRID_W - NA_WIN_W)
    col_in = (col[None, :] >= col_start[:, None]) & (col[None, :] < col_start[:, None] + NA_WIN_W)
    period = 2 * GRID_W
    n_r = 2 * NA_WIN_H - 1
    vec = jnp.zeros((NA_HEADS, n_r, period), F32)
    vec = vec.at[..., :NA_WIN_W].set(rpb[..., NA_WIN_W - 1:])
    vec = vec.at[..., period - (NA_WIN_W - 1):].set(rpb[..., :NA_WIN_W - 1])
    toep = jnp.tile(vec, (1, 1, GRID_W))[..., :GRID_W * (period - 1)]
    toep = toep.reshape(NA_HEADS, n_r, GRID_W, period - 1)[..., :GRID_W]
    b = jnp.stack([toep[:, NA_WIN_H - 1 - d:2 * NA_WIN_H - 1 - d] for d in range(NA_WIN_H)], axis=1)
    b = jnp.where(col_in[None, None, None], b, NEG_BIG)
    b = b.transpose(0, 1, 3, 2, 4).reshape(NA_HEADS, NA_WIN_H, GRID_W, NA_KEYS)
    b = b.reshape(NA_HEADS // NA_GROUP, NA_GROUP, NA_WIN_H, GRID_W, NA_KEYS).transpose(0, 2, 1, 3, 4)
    return b.reshape(NA_HEADS // NA_GROUP, NA_WIN_H, NA_LANES, NA_KEYS).astype(F32)


def _block_diag(w):
    per = BD_TILE // QKV_BLOCK
    wt = w.transpose(0, 2, 1).reshape(N_BD, per, QKV_BLOCK, QKV_BLOCK)
    eye = jnp.eye(per, dtype=w.dtype)
    return jnp.einsum('caio,ab->caibo', wt, eye).reshape(N_BD, BD_TILE, BD_TILE)


def _prep(g_norm, g_final, w_in0, g_qlat, g_kvlat, w_uq, w_ukv, na_rpb, w_out0,
          w_in1, conv_w, conv_b, w_q, w_k, w_v, w_gate, b_gate, g_mh, skip, w_out1):
    p = {}
    w0 = w_in0[0]
    o = np.cumsum((Q_LORA, KV_LORA, QK_ROPE, NA_WIDTH, NA_WIDTH, NA_WIDTH))
    kr = jnp.zeros((D_MODEL, HEAD_PAD), F32).at[:, QK_NOPE:QK_NOPE + QK_ROPE].set(w0[:, o[1]:o[2]])
    p['wa'] = jnp.concatenate([w0[:, :o[1]], kr], axis=-1).astype(BF16)
    p['wb'] = w0[:, o[2]:].astype(BF16)
    wq = w_uq[0].reshape(Q_LORA, MLA_HEADS, QK_NOPE + QK_ROPE)
    p['wuq'] = jnp.pad(wq, ((0, 0), (0, 0), (0, HEAD_PAD - QK_NOPE - QK_ROPE))).reshape(Q_LORA, MLA_QK).astype(BF16)
    wkv = w_ukv[0].reshape(KV_LORA, MLA_HEADS, QK_NOPE + V_DIM)
    wk = jnp.pad(wkv[:, :, :QK_NOPE], ((0, 0), (0, 0), (0, HEAD_PAD - QK_NOPE))).reshape(KV_LORA, MLA_QK)
    wv = jnp.pad(wkv[:, :, QK_NOPE:], ((0, 0), (0, 0), (0, HEAD_PAD - V_DIM))).reshape(KV_LORA, MLA_QK)
    p['wukv'] = jnp.concatenate([wk, wv], axis=-1).astype(BF16)
    p['gq'] = g_qlat[0].reshape(1, Q_LORA)
    p['gkv'] = g_kvlat[0].reshape(1, KV_LORA)
    p['g0'] = g_norm[0].reshape(1, D_MODEL)
    p['g1'] = g_norm[1].reshape(1, D_MODEL)
    p['gf'] = g_final.reshape(1, D_MODEL)
    p['bias'] = _na_bias(na_rpb[0])
    p['wo0'] = w_out0[0].astype(BF16)
    p['wi1'] = w_in1[0].astype(BF16)
    p['cw'] = conv_w[0]
    p['cb'] = conv_b[0].reshape(1, ML_INNER)
    gate_perm = np.array([d * 2 * ML_HEADS + h for h in range(ML_HEADS) for d in range(2)]
                         + [d * 2 * ML_HEADS + ML_HEADS + h for h in range(ML_HEADS) for d in range(2)])
    wg = w_gate[0][:, gate_perm].reshape(3, ML_INNER // QKV_BLOCK, QKV_BLOCK, 2 * GATE_ROWS)

    def fold(w, g):
        f = jnp.einsum('noi,nog->nig', w, g, precision=lax.Precision.HIGHEST).reshape(ML_INNER, 2 * GATE_ROWS)
        return jnp.pad(f, ((0, 0), (0, LANES - 2 * GATE_ROWS))).reshape(N_BD, BD_TILE, LANES)

    p['bdqk'] = jnp.concatenate([_block_diag(w_q[0]), _block_diag(w_k[0]),
                                 fold(w_q[0], wg[0]) + fold(w_k[0], wg[1])], axis=-1).astype(BF16)
    p['bdv'] = jnp.concatenate([_block_diag(w_v[0]), fold(w_v[0], wg[2])], axis=-1).astype(BF16)
    p['bg'] = jnp.pad(b_gate[0][gate_perm].reshape(1, -1), ((0, 0), (0, LANES - 2 * GATE_ROWS)))
    p['gmh'] = g_mh[0].reshape(1, ML_INNER)
    p['skip'] = skip[0].reshape(1, ML_INNER)
    p['wo1'] = w_out1[0].astype(BF16)
    return p


def _trunk(x, mod0, mod1, p, tabs, tm=512):
    B, S, _ = x.shape
    rc, rm, rp = tabs
    q, k, v, nq, nk, nv, gate = _in0(x, mod0, p['g0'], p['wa'], p['wb'], p['gq'], p['gkv'],
                                      p['wuq'], p['wukv'], rc, rm, rp, tm=tm)
    o_mla = _mla(q, k, v)
    o_na = _natten(nq, nk, nv, p['bias'])
    x1, xm, z = _mid(o_mla, o_na, gate, x, mod0, mod1, p['g1'], p['wo0'], p['wi1'], tm=tm)
    mq, mk, mv, xc, pre = _conv(xm, p['cw'], p['cb'], p['bdqk'], p['bdv'], p['bg'], tm=tm)
    gc, gr = _gates(pre)
    hn = _mlstm(mq, mk, mv, gc, gr, p['gmh'])
    return _out(hn, xc, z, x1, mod1, p['skip'], p['wo1'], p['gf'], tm=tm)


def kernel(x_prompt, x_sample, c_prompt, c_sample, g_norm, w_ada, b_ada, g_final, w_in0, g_qlat, g_kvlat, w_uq, w_ukv, na_rpb, w_out0, w_in1, conv_w, conv_b, w_q, w_k, w_v, w_gate, b_gate, g_mh, skip, w_out1):
    p = _prep(g_norm, g_final, w_in0, g_qlat, g_kvlat, w_uq, w_ukv, na_rpb, w_out0,
              w_in1, conv_w, conv_b, w_q, w_k, w_v, w_gate, b_gate, g_mh, skip, w_out1)
    nb_p = x_prompt.shape[0]
    mod = _ada(jnp.concatenate([c_prompt, c_sample], axis=0), w_ada, b_ada)
    mod = mod.reshape(DEPTH, -1, 3, D_MODEL)
    outs = []
    for x, sl in ((x_prompt, slice(0, nb_p)), (x_sample, slice(nb_p, None))):
        tabs = _rope_tables(x.shape[1])
        outs.append(_trunk(x, mod[0, sl], mod[1, sl], p, tabs))
    return tuple(outs)
```

```python
import functools

import numpy as np
import jax
import jax.numpy as jnp
from jax import lax
from jax.experimental import pallas as pl
from jax.experimental.pallas import tpu as pltpu

F32 = jnp.float32
BF16 = jnp.bfloat16

D_MODEL = 1024
DEPTH = 2
GRID_W = 64
EPS = 1e-6
MLA_HEADS = 8
QK_NOPE = 64
QK_ROPE = 32
V_DIM = 64
Q_LORA = 256
KV_LORA = 256
ROPE_THETA = 10000.0
MLA_WIDTH = MLA_HEADS * V_DIM
NA_HEADS = 8
NA_DIM = 64
NA_WIN_H = 8
NA_WIN_W = 16
NA_WIDTH = NA_HEADS * NA_DIM
ML_HEADS = 4
ML_INNER = 2 * D_MODEL
ML_DIM = ML_INNER // ML_HEADS
ML_CONV = 5
QKV_BLOCK = 4
ML_L = 256

HEAD_PAD = 128
MLA_QK = MLA_HEADS * HEAD_PAD
ROPE_HALF = QK_ROPE // 2
LANES = 128
BD_TILE = 256
N_BD = ML_INNER // BD_TILE
NEG_BIG = -1e30
LOG2E = 1.4426950408889634

VMEM_LIMIT = 56 * 1024 * 1024

_NT = (((1,), (1,)), ((), ()))
_TN = (((0,), (0,)), ((), ()))


def _cparams(sem):
    return pltpu.CompilerParams(dimension_semantics=sem, vmem_limit_bytes=VMEM_LIMIT)


def _silu(x):
    return x * jax.nn.sigmoid(x)


def _rms(x, g):
    return x * lax.rsqrt(jnp.mean(x * x, axis=-1, keepdims=True) + EPS) * g


def _ada_kernel(c_ref, w_ref, b_ref, o_ref):
    c = c_ref[...]
    o_ref[0] = jnp.dot(_silu(c).astype(BF16), w_ref[0].astype(BF16), preferred_element_type=F32) + b_ref[0]


def _ada(c_all, w_ada, b_ada):
    nb = c_all.shape[0]
    tn = 512
    return pl.pallas_call(
        _ada_kernel,
        grid=(DEPTH, 3 * D_MODEL // tn),
        in_specs=[pl.BlockSpec((nb, D_MODEL), lambda l, j: (0, 0)),
                  pl.BlockSpec((1, D_MODEL, tn), lambda l, j: (l, 0, j)),
                  pl.BlockSpec((1, 1, tn), lambda l, j: (l, 0, j))],
        out_specs=pl.BlockSpec((1, nb, tn), lambda l, j: (l, 0, j)),
        out_shape=jax.ShapeDtypeStruct((DEPTH, nb, 3 * D_MODEL), F32),
        compiler_params=_cparams(("arbitrary", "arbitrary")),
        name="ada",
    )(c_all, w_ada, b_ada.reshape(DEPTH, 1, 3 * D_MODEL))


def _in0_kernel(x_ref, mod_ref, g_ref, wa_ref, wb_ref, gq_ref, gkv_ref, wuq_ref, wukv_ref,
                rc_ref, rm_ref, rp_ref,
                q_ref, k_ref, v_ref, nq_ref, nk_ref, nv_ref, gate_ref):
    x = x_ref[0]
    shift = mod_ref[0, 0:1, :]
    scale = mod_ref[0, 1:2, :]
    h = _rms(x, g_ref[...]) * (1.0 + scale) + shift
    hb = h.astype(BF16)

    ua = jnp.dot(hb, wa_ref[...], preferred_element_type=F32)
    qn = _rms(ua[:, :Q_LORA], gq_ref[...]).astype(BF16)
    kvn = _rms(ua[:, Q_LORA:Q_LORA + KV_LORA], gkv_ref[...]).astype(BF16)
    kr = ua[:, Q_LORA + KV_LORA:]
    qf = jnp.dot(qn, wuq_ref[...], preferred_element_type=F32)
    kvf = jnp.dot(kvn, wukv_ref[...], preferred_element_type=F32)

    rc, rm, rp = rc_ref[...], rm_ref[...], rp_ref[...]

    def rope(t):
        return (t * rc + pltpu.roll(t, LANES - ROPE_HALF, 1) * rm + pltpu.roll(t, ROPE_HALF, 1) * rp)

    krr = rope(kr)
    qscale = (QK_NOPE + QK_ROPE) ** -0.5 * LOG2E
    for hh in range(MLA_HEADS):
        sl = slice(hh * HEAD_PAD, (hh + 1) * HEAD_PAD)
        q_ref[0, :, sl] = (rope(qf[:, sl]) * qscale).astype(BF16)
        k_ref[0, :, sl] = (kvf[:, sl] + krr).astype(BF16)
    ones_pad = (lax.broadcasted_iota(jnp.int32, (1, MLA_QK), 1) % HEAD_PAD >= V_DIM).astype(F32)
    v_ref[0] = (kvf[:, MLA_QK:] + ones_pad).astype(BF16)

    ub = jnp.dot(hb, wb_ref[...], preferred_element_type=F32)
    nq_ref[0] = (ub[:, :NA_WIDTH] * (NA_DIM ** -0.5 * LOG2E)).astype(BF16)
    nk_ref[0] = ub[:, NA_WIDTH:2 * NA_WIDTH].astype(BF16)
    nv_ref[0] = ub[:, 2 * NA_WIDTH:3 * NA_WIDTH].astype(BF16)
    gate_ref[0] = ub[:, 3 * NA_WIDTH:].astype(BF16)


def _in0(x, mod, g, wa, wb, gq, gkv, wuq, wukv, rc, rm, rp, tm=512):
    B, S, D = x.shape
    full = lambda a: pl.BlockSpec(a.shape, lambda b, i: (0,) * a.ndim)
    tok = lambda w: pl.BlockSpec((1, tm, w), lambda b, i: (b, i, 0))
    tab = pl.BlockSpec((tm, LANES), lambda b, i: (i, 0))
    widths = (MLA_QK, MLA_QK, MLA_QK, NA_WIDTH, NA_WIDTH, NA_WIDTH, MLA_WIDTH + NA_WIDTH)
    return pl.pallas_call(
        _in0_kernel,
        grid=(B, S // tm),
        in_specs=[tok(D), pl.BlockSpec((1, 3, D), lambda b, i: (b, 0, 0)), full(g), full(wa), full(wb),
                  full(gq), full(gkv), full(wuq), full(wukv), tab, tab, tab],
        out_specs=[tok(w) for w in widths],
        out_shape=[jax.ShapeDtypeStruct((B, S, w), BF16) for w in widths],
        compiler_params=_cparams(("parallel", "parallel")),
        name="in0",
    )(x, mod, g, wa, wb, gq, gkv, wuq, wukv, rc, rm, rp)


def _mla_kernel(q_ref, k_ref, v_ref, o_ref):
    low = lax.broadcasted_iota(jnp.int32, (q_ref.shape[1], HEAD_PAD), 1) < V_DIM

    def scores(hh):
        sl = slice(hh * HEAD_PAD, (hh + 1) * HEAD_PAD)
        return lax.dot_general(q_ref[0, :, sl], k_ref[0, :, sl], _NT, preferred_element_type=F32)

    s_next = scores(0)
    outs = []
    for hh in range(MLA_HEADS):
        s = s_next
        if hh + 1 < MLA_HEADS:
            s_next = scores(hh + 1)
        sl = slice(hh * HEAD_PAD, (hh + 1) * HEAD_PAD)
        p = jnp.exp2(s - jnp.max(s, axis=-1, keepdims=True))
        ol = jnp.dot(p.astype(BF16), v_ref[0, :, sl], preferred_element_type=F32)
        outs.append(ol / jnp.where(low, pltpu.roll(ol, V_DIM, 1), 1.0))
        if hh % 2 == 1:
            vs = slice((hh - 1) * V_DIM, (hh + 1) * V_DIM)
            o_ref[0, :, vs] = jnp.where(low, outs[hh - 1], pltpu.roll(outs[hh], V_DIM, 1)).astype(BF16)


def _mla(q, k, v, tq=512):
    B, S, _ = q.shape
    return pl.pallas_call(
        _mla_kernel,
        grid=(B, S // tq),
        in_specs=[pl.BlockSpec((1, tq, MLA_QK), lambda b, i: (b, i, 0)),
                  pl.BlockSpec((1, S, MLA_QK), lambda b, i: (b, 0, 0)),
                  pl.BlockSpec((1, S, MLA_QK), lambda b, i: (b, 0, 0))],
        out_specs=pl.BlockSpec((1, tq, MLA_WIDTH), lambda b, i: (b, i, 0)),
        out_shape=jax.ShapeDtypeStruct((B, S, MLA_WIDTH), BF16),
        compiler_params=_cparams(("parallel", "arbitrary")),
        name="mla",
    )(q, k, v)


NA_GROUP = 4
NA_LANES = NA_GROUP * NA_DIM
NA_KEYS = NA_WIN_H * GRID_W
NA_UNROLL = 8


def _na_kernel(q_ref, k_ref, v_ref, bias_ref, o_ref, *, rows):
    rh = lax.broadcasted_iota(jnp.int32, (NA_LANES, NA_LANES), 0) // GRID_W
    lh = lax.broadcasted_iota(jnp.int32, (NA_LANES, NA_LANES), 1) // NA_DIM
    hmask = (rh == lh).astype(F32)

    def window(r):
        r0 = jnp.clip(r - NA_WIN_H // 2, 0, rows - NA_WIN_H)
        return r - r0, pl.ds(pl.multiple_of(r0 * GRID_W, GRID_W), NA_KEYS)

    def scores(r):
        d, win = window(r)
        qr = q_ref[0, pl.ds(pl.multiple_of(r * GRID_W, GRID_W), GRID_W), :].astype(F32)
        qs = (jnp.concatenate([qr] * NA_GROUP, axis=0) * hmask).astype(BF16)
        return lax.dot_general(qs, k_ref[0, win, :], _NT, preferred_element_type=F32) + bias_ref[0, d]

    def body(t, carry):
        base = t * NA_UNROLL
        s_next = scores(base)
        for i in range(NA_UNROLL):
            r = base + i
            s = s_next
            if i + 1 < NA_UNROLL:
                s_next = scores(r + 1)
            _, win = window(r)
            p = jnp.exp2(s - jnp.max(s, axis=-1, keepdims=True))
            l = jnp.sum(p, axis=-1, keepdims=True)
            o = jnp.dot(p.astype(BF16), v_ref[0, win, :], preferred_element_type=F32) / l * hmask
            out = o[0:GRID_W]
            for g in range(1, NA_GROUP):
                out = out + o[g * GRID_W:(g + 1) * GRID_W]
            o_ref[0, pl.ds(pl.multiple_of(r * GRID_W, GRID_W), GRID_W), :] = out.astype(BF16)
        return carry

    lax.fori_loop(0, rows // NA_UNROLL, body, 0)


def _natten(q, k, v, bias):
    B, S, _ = q.shape
    rows = S // GRID_W
    tok = pl.BlockSpec((1, S, NA_LANES), lambda g, b: (b, 0, g))
    return pl.pallas_call(
        functools.partial(_na_kernel, rows=rows),
        grid=(NA_HEADS // NA_GROUP, B),
        in_specs=[tok, tok, tok,
                  pl.BlockSpec((1, NA_WIN_H, NA_LANES, NA_KEYS), lambda g, b: (g, 0, 0, 0))],
        out_specs=tok,
        out_shape=jax.ShapeDtypeStruct((B, S, NA_WIDTH), BF16),
        compiler_params=_cparams(("parallel", "parallel")),
        name="natten",
    )(q, k, v, bias)


def _mid_kernel(om_ref, on_ref, gate_ref, x_ref, mod0_ref, mod1_ref, g_ref, wo_ref, wi_ref,
                x1_ref, xm_ref, z_ref):
    sg = _silu(gate_ref[0].astype(F32))
    o = jnp.concatenate([om_ref[0].astype(F32), on_ref[0].astype(F32)], axis=-1) * sg
    out = jnp.dot(o.astype(BF16), wo_ref[...], preferred_element_type=F32)
    x1 = x_ref[0] + mod0_ref[0, 2:3, :] * out
    x1_ref[0] = x1
    h = _rms(x1, g_ref[...]) * (1.0 + mod1_ref[0, 1:2, :]) + mod1_ref[0, 0:1, :]
    u = jnp.dot(h.astype(BF16), wi_ref[...], preferred_element_type=F32)
    xm_ref[0] = u[:, :ML_INNER].astype(BF16)
    z_ref[0] = u[:, ML_INNER:].astype(BF16)


def _mid(om, on, gate, x, mod0, mod1, g, wo, wi, tm=512):
    B, S, D = x.shape
    full = lambda a: pl.BlockSpec(a.shape, lambda b, i: (0,) * a.ndim)
    tok = lambda w: pl.BlockSpec((1, tm, w), lambda b, i: (b, i, 0))
    modspec = pl.BlockSpec((1, 3, D), lambda b, i: (b, 0, 0))
    return pl.pallas_call(
        _mid_kernel,
        grid=(B, S // tm),
        in_specs=[tok(MLA_WIDTH), tok(NA_WIDTH), tok(D), tok(D), modspec, modspec, full(g), full(wo), full(wi)],
        out_specs=[tok(D), tok(ML_INNER), tok(ML_INNER)],
        out_shape=[jax.ShapeDtypeStruct((B, S, D), F32),
                   jax.ShapeDtypeStruct((B, S, ML_INNER), BF16),
                   jax.ShapeDtypeStruct((B, S, ML_INNER), BF16)],
        compiler_params=_cparams(("parallel", "parallel")),
        name="mid",
    )(om, on, gate, x, mod0, mod1, g, wo, wi)


HALO = 16
SHIFT_ROWS = 128
SHIFT_WIN = SHIFT_ROWS + 2 * HALO
GATE_ROWS = 2 * ML_HEADS
PADR = 8
MXU_TAPS = (ML_CONV // 2 - 1, ML_CONV // 2 + 1)
VALU_TAPS = tuple(j for j in range(ML_CONV) if j not in MXU_TAPS)


def _log_sigmoid(x):
    return jnp.minimum(x, 0.0) - jnp.log1p(jnp.exp(-jnp.abs(x)))


def _chunk_scan(x, pos, op, ident, reverse):
    n = x.shape[0]
    k = 1
    while k < ML_L:
        if reverse:
            x = op(x, jnp.where(pos < ML_L - k, pltpu.roll(x, n - k, 0), ident))
        else:
            x = op(x, jnp.where(pos >= k, pltpu.roll(x, k, 0), ident))
        k *= 2
    return x


def _conv_kernel(xm_ref, xp_ref, xn_ref, sh_ref, cw_ref, cb_ref, bdqk_ref, bdv_ref, bg_ref,
                 q_ref, k_ref, v_ref, xc_ref, pre_ref, ext_ref, *, tm):
    i = pl.program_id(1)
    n = pl.num_programs(1)
    ext_ref[0:PADR, :] = jnp.where(i > 0, xp_ref[0].astype(F32)[HALO - PADR:, :], 0.0)
    ext_ref[PADR:PADR + tm, :] = xm_ref[0].astype(F32)
    ext_ref[PADR + tm:, :] = jnp.where(i < n - 1, xn_ref[0].astype(F32)[:PADR, :], 0.0)
    zero_halo = jnp.zeros((HALO, BD_TILE), BF16)
    qscale = ML_DIM ** -0.5
    pre = jnp.zeros((tm, LANES), F32) + bg_ref[...]

    def shifted(c):
        cs = slice(c * BD_TILE, (c + 1) * BD_TILE)
        prev = jnp.where(i > 0, xp_ref[0, :, cs], zero_halo)
        nxt = jnp.where(i < n - 1, xn_ref[0, :, cs], zero_halo)
        out = []
        for rb in range(tm // SHIFT_ROWS):
            lo, hi = rb * SHIFT_ROWS - HALO, (rb + 1) * SHIFT_ROWS + HALO
            parts = [prev] if lo < 0 else []
            parts.append(xm_ref[0, max(lo, 0):min(hi, tm), cs])
            if hi > tm:
                parts.append(nxt)
            win = jnp.concatenate(parts, axis=0) if len(parts) > 1 else parts[0]
            out.append(jnp.dot(sh_ref[...], win, preferred_element_type=F32))
        return out

    sh_next = shifted(0)
    for c in range(N_BD):
        cs = slice(c * BD_TILE, (c + 1) * BD_TILE)
        sh_cur = sh_next
        ys = []
        for rb in range(tm // SHIFT_ROWS):
            y = cb_ref[:, cs]
            for t, j in enumerate(MXU_TAPS):
                y = y + sh_cur[rb][t * SHIFT_ROWS:(t + 1) * SHIFT_ROWS] * cw_ref[j:j + 1, cs]
            for j in VALU_TAPS:
                r0 = PADR + rb * SHIFT_ROWS + j - ML_CONV // 2
                y = y + ext_ref[r0:r0 + SHIFT_ROWS, cs] * cw_ref[j:j + 1, cs]
            ys.append(y)
        xc = _silu(jnp.concatenate(ys, axis=0))
        xcb = xc.astype(BF16)
        xc_ref[0, :, cs] = xcb
        if c + 1 < N_BD:
            sh_next = shifted(c + 1)
        qk = jnp.dot(xcb, bdqk_ref[c], preferred_element_type=F32)
        vv = jnp.dot(xm_ref[0, :, cs], bdv_ref[c], preferred_element_type=F32)
        pre = pre + qk[:, 2 * BD_TILE:] + vv[:, BD_TILE:]
        q_ref[0, :, cs] = (qk[:, :BD_TILE] * qscale).astype(BF16)
        k_ref[0, :, cs] = qk[:, BD_TILE:2 * BD_TILE].astype(BF16)
        v_ref[0, :, cs] = vv[:, :BD_TILE].astype(BF16)

    pre_ref[0] = pre


def _gates_kernel(pre_ref, gc_ref, gr_ref):
    pre = pre_ref[0]
    lane = lax.broadcasted_iota(jnp.int32, pre.shape, 1)
    pos = lax.broadcasted_iota(jnp.int32, pre.shape, 0) % ML_L
    fwd = lane % 2 == 0
    lf = _log_sigmoid(pre)
    b = jnp.where(fwd, _chunk_scan(lf, pos, jnp.add, 0.0, False), _chunk_scan(lf, pos, jnp.add, 0.0, True))
    a = pre - pltpu.roll(b, LANES - GATE_ROWS, 1)
    c = jnp.where(fwd, _chunk_scan(a, pos, jnp.maximum, -jnp.inf, False),
                  _chunk_scan(a, pos, jnp.maximum, -jnp.inf, True))
    g = jnp.where(lane < GATE_ROWS, a, jnp.where(lane < 2 * GATE_ROWS, b, pltpu.roll(c, 2 * GATE_ROWS, 1)))
    gc_ref[0] = g
    gr_ref[0] = g.T[:3 * GATE_ROWS, :]


def _shift_matrix():
    m = np.zeros((len(MXU_TAPS) * SHIFT_ROWS, SHIFT_WIN), np.float32)
    r = np.arange(SHIFT_ROWS)
    for t, j in enumerate(MXU_TAPS):
        m[t * SHIFT_ROWS + r, r + HALO + j - ML_CONV // 2] = 1.0
    return jnp.asarray(m, dtype=BF16)


def _conv(xm, cw, cb, bdqk, bdv, bg, tm=512):
    sh = _shift_matrix()
    B, S, _ = xm.shape
    hb = tm // HALO
    full = lambda a: pl.BlockSpec(a.shape, lambda b, i: (0,) * a.ndim)
    tok = lambda w: pl.BlockSpec((1, tm, w), lambda b, i: (b, i, 0))
    return pl.pallas_call(
        functools.partial(_conv_kernel, tm=tm),
        grid=(B, S // tm),
        in_specs=[tok(ML_INNER),
                  pl.BlockSpec((1, HALO, ML_INNER), lambda b, i: (b, jnp.maximum(i * hb - 1, 0), 0)),
                  pl.BlockSpec((1, HALO, ML_INNER), lambda b, i: (b, jnp.minimum((i + 1) * hb, S // HALO - 1), 0)),
                  full(sh), full(cw), full(cb), full(bdqk), full(bdv), full(bg)],
        out_specs=[tok(ML_INNER), tok(ML_INNER), tok(ML_INNER), tok(ML_INNER), tok(LANES)],
        out_shape=[jax.ShapeDtypeStruct((B, S, ML_INNER), BF16)] * 4 + [jax.ShapeDtypeStruct((B, S, LANES), F32)],
        scratch_shapes=[pltpu.VMEM((tm + 2 * PADR, ML_INNER), F32)],
        compiler_params=_cparams(("parallel", "parallel")),
        name="conv",
    )(xm, xm, xm, sh, cw, cb, bdqk, bdv, bg)


def _gates(pre):
    B, S, _ = pre.shape
    return pl.pallas_call(
        _gates_kernel,
        grid=(B,),
        in_specs=[pl.BlockSpec((1, S, LANES), lambda b: (b, 0, 0))],
        out_specs=[pl.BlockSpec((1, S, LANES), lambda b: (b, 0, 0)),
                   pl.BlockSpec((1, 3 * GATE_ROWS, S), lambda b: (b, 0, 0))],
        out_shape=[jax.ShapeDtypeStruct((B, S, LANES), F32), jax.ShapeDtypeStruct((B, 3 * GATE_ROWS, S), F32)],
        compiler_params=_cparams(("parallel",)),
        name="gates",
    )(pre)


ML_UNROLL = 2


def _mlstm_kernel(q_ref, k_ref, v_ref, gc_ref, gr_ref, gmh_ref, o_ref,
                  cf_ref, cb_ref, sf_ref, sb_ref, nst_ref, mf_ref, mb_ref, mrep_ref, brep_ref, *, seq):
    L = ML_L
    nc = seq // L
    head = pl.program_id(1)
    cf_ref[...] = jnp.zeros_like(cf_ref)
    cb_ref[...] = jnp.zeros_like(cb_ref)
    nst_ref[...] = jnp.zeros_like(nst_ref)
    lane = lax.broadcasted_iota(jnp.int32, (L, LANES), 1)

    def wide(x, n):
        return jnp.concatenate([x] * n, axis=1)

    def state_step(c, d, ct_ref, st_ref, mst_ref, nvec, m):
        rows = pl.ds(pl.multiple_of(c * L, L), L)
        r = head * 2 + d
        kc, vc = k_ref[0, rows, :], v_ref[0, rows, :]
        gcol = gc_ref[0, rows, :]
        a_col, b_col, c_col = [jnp.sum(jnp.where(lane == g * GATE_ROWS + r, gcol, 0.0), axis=1, keepdims=True)
                               for g in range(3)]
        a_row = gr_ref[0, pl.ds(r, 1), rows]
        st_ref[c] = ct_ref[...].astype(BF16)
        nst_ref[c, d:d + 1, :] = nvec
        mst_ref[c] = jnp.broadcast_to(m, mst_ref.shape[1:])
        mrep = jnp.maximum(c_col, m)
        mrep_ref[d, rows, :] = jnp.broadcast_to(mrep, (L, LANES))
        brep_ref[d, rows, :] = jnp.broadcast_to(b_col + mrep, (L, LANES))
        end = L - 1 if d == 0 else 0
        b_last, c_last = b_col[end:end + 1, :], c_col[end:end + 1, :]
        m_new = b_last + jnp.maximum(m, c_last)
        w_old = jnp.exp(b_last + m - m_new)
        w_s = jnp.exp(b_last + a_col - m_new)
        w_row = jnp.exp(b_last + a_row - m_new)
        wv = (w_s * vc.astype(F32)).astype(BF16)
        ct_ref[...] = w_old * ct_ref[...] + lax.dot_general(kc, wv, _TN, preferred_element_type=F32)
        dn = jnp.dot(jnp.broadcast_to(w_row, (8, L)).astype(BF16), kc, preferred_element_type=F32)
        return w_old * nvec + dn[0:1], m_new

    def scan_body(j, carry):
        nf, mf, nb, mb = carry
        nf, mf = state_step(j, 0, cf_ref, sf_ref, mf_ref, nf, mf)
        nb, mb = state_step(nc - 1 - j, 1, cb_ref, sb_ref, mb_ref, nb, mb)
        return nf, mf, nb, mb

    z_n = jnp.zeros((1, ML_DIM), F32)
    z_m = jnp.zeros((1, 1), F32)
    lax.fori_loop(0, nc, scan_body, (z_n, z_m, z_n, z_m), unroll=ML_UNROLL)

    ri = lax.broadcasted_iota(jnp.int32, (L, L), 0)
    ci = lax.broadcasted_iota(jnp.int32, (L, L), 1)

    def decay(d, rows, m, mask):
        a_row = gr_ref[0, pl.ds(head * 2 + d, 1), rows]
        mrep = mrep_ref[d, rows, :]
        dmat = jnp.exp(jnp.where(mask, a_row - wide(mrep, L // LANES), -jnp.inf))
        return dmat, jnp.exp(m - mrep), jnp.exp(-brep_ref[d, rows, :])

    def out_body(c, carry):
        rows = pl.ds(pl.multiple_of(c * L, L), L)
        qc, kc, vc = q_ref[0, rows, :], k_ref[0, rows, :], v_ref[0, rows, :]
        kn = jnp.concatenate([kc, nst_ref[c].astype(BF16)], axis=0)
        sx = lax.dot_general(qc, kn, _NT, preferred_element_type=F32)
        hf = jnp.dot(qc, sf_ref[c], preferred_element_type=F32)
        hb = jnp.dot(qc, sb_ref[c], preferred_element_type=F32)
        d_f, wi_f, e_f = decay(0, rows, mf_ref[c][0:1, 0:1], ci <= ri)
        d_b, wi_b, e_b = decay(1, rows, mb_ref[c][0:1, 0:1], ci >= ri)
        s = sx[:, :L]
        s_f, s_b = s * d_f, s * d_b
        qn_f = jnp.broadcast_to(sx[:, L:L + 1], (L, LANES))
        qn_b = jnp.broadcast_to(sx[:, L + 1:L + 2], (L, LANES))
        r_f = 1.0 / jnp.maximum(jnp.abs(wi_f * qn_f + jnp.sum(s_f, axis=-1, keepdims=True)), e_f)
        r_b = 1.0 / jnp.maximum(jnp.abs(wi_b * qn_b + jnp.sum(s_b, axis=-1, keepdims=True)), e_b)
        p = (s_f * wide(r_f, L // LANES) + s_b * wide(r_b, L // LANES)).astype(BF16)
        hs = (hf * wide(wi_f * r_f, ML_DIM // LANES) + hb * wide(wi_b * r_b, ML_DIM // LANES)
              + jnp.dot(p, vc, preferred_element_type=F32))
        mu = jnp.mean(hs, axis=-1, keepdims=True)
        var = jnp.mean(jnp.square(hs - mu), axis=-1, keepdims=True)
        o_ref[0, rows, :] = ((hs - mu) * lax.rsqrt(var + EPS) * gmh_ref[...]).astype(BF16)
        return carry

    lax.fori_loop(0, nc, out_body, 0, unroll=2 * ML_UNROLL)


def _mlstm(q, k, v, gc, gr, gmh):
    B, S, _ = q.shape
    nc = S // ML_L
    tok = pl.BlockSpec((1, S, ML_DIM), lambda b, h: (b, 0, h))
    return pl.pallas_call(
        functools.partial(_mlstm_kernel, seq=S),
        grid=(B, ML_HEADS),
        in_specs=[tok, tok, tok,
                  pl.BlockSpec((1, S, LANES), lambda b, h: (b, 0, 0)),
                  pl.BlockSpec((1, 3 * GATE_ROWS, S), lambda b, h: (b, 0, 0)),
                  pl.BlockSpec((1, ML_DIM), lambda b, h: (0, h))],
        out_specs=tok,
        out_shape=jax.ShapeDtypeStruct((B, S, ML_INNER), BF16),
        scratch_shapes=[pltpu.VMEM((ML_DIM, ML_DIM), F32), pltpu.VMEM((ML_DIM, ML_DIM), F32),
                        pltpu.VMEM((nc, ML_DIM, ML_DIM), BF16), pltpu.VMEM((nc, ML_DIM, ML_DIM), BF16),
                        pltpu.VMEM((nc, 16, ML_DIM), F32),
                        pltpu.VMEM((nc, 8, LANES), F32), pltpu.VMEM((nc, 8, LANES), F32),
                        pltpu.VMEM((2, S, LANES), F32), pltpu.VMEM((2, S, LANES), F32)],
        compiler_params=_cparams(("parallel", "arbitrary")),
        name="mlstm",
    )(q, k, v, gc, gr, gmh)


def _out_kernel(hn_ref, xc_ref, z_ref, x1_ref, mod1_ref, skip_ref, wo_ref, gf_ref, y_ref):
    o = (hn_ref[0].astype(F32) + skip_ref[...] * xc_ref[0].astype(F32)) * _silu(z_ref[0].astype(F32))
    out = jnp.dot(o.astype(BF16), wo_ref[...], preferred_element_type=F32)
    x2 = x1_ref[0] + mod1_ref[0, 2:3, :] * out
    y_ref[0] = _rms(x2, gf_ref[...])


def _out(hn, xc, z, x1, mod1, skip, wo, gf, tm=512):
    B, S, D = x1.shape
    full = lambda a: pl.BlockSpec(a.shape, lambda b, i: (0,) * a.ndim)
    tok = lambda w: pl.BlockSpec((1, tm, w), lambda b, i: (b, i, 0))
    return pl.pallas_call(
        _out_kernel,
        grid=(B, S // tm),
        in_specs=[tok(ML_INNER), tok(ML_INNER), tok(ML_INNER), tok(D),
                  pl.BlockSpec((1, 3, D), lambda b, i: (b, 0, 0)), full(skip), full(wo), full(gf)],
        out_specs=tok(D),
        out_shape=jax.ShapeDtypeStruct((B, S, D), F32),
        compiler_params=_cparams(("parallel", "parallel")),
        name="out",
    )(hn, xc, z, x1, mod1, skip, wo, gf)


def _rope_tables(S):
    pos = jnp.arange(S, dtype=F32)
    inv = 1.0 / (ROPE_THETA ** (jnp.arange(0, QK_ROPE, 2, dtype=F32) / QK_ROPE))
    ang = pos[:, None] * inv[None, :]
    cos, sin = jnp.cos(ang), jnp.sin(ang)
    ones = jnp.ones((S, QK_NOPE), F32)
    z16 = jnp.zeros((S, ROPE_HALF), F32)
    z32 = jnp.zeros((S, HEAD_PAD - QK_NOPE - QK_ROPE), F32)
    z64 = jnp.zeros((S, QK_NOPE), F32)
    rc = jnp.concatenate([ones, cos, cos, z32], axis=-1)
    rm = jnp.concatenate([z64, -sin, z16, z32], axis=-1)
    rp = jnp.concatenate([z64, z16, sin, z32], axis=-1)
    return rc, rm, rp


def _na_bias(rpb):
    col = np.arange(GRID_W)
    col_start = np.clip(col - NA_WIN_W // 2, 0, GRID_W - NA_WIN_W)
    col_in = (col[None, :] >= col_start[:, None]) & (col[None, :] < col_start[:, None] + NA_WIN_W)
    period = 2 * GRID_W
    n_r = 2 * NA_WIN_H - 1
    vec = jnp.zeros((NA_HEADS, n_r, period), F32)
    vec = vec.at[..., :NA_WIN_W].set(rpb[..., NA_WIN_W - 1:])
    vec = vec.at[..., period - (NA_WIN_W - 1):].set(rpb[..., :NA_WIN_W - 1])
    toep = jnp.tile(vec, (1, 1, GRID_W))[..., :GRID_W * (period - 1)]
    toep = toep.reshape(NA_HEADS, n_r, GRID_W, period - 1)[..., :GRID_W]
    b = jnp.stack([toep[:, NA_WIN_H - 1 - d:2 * NA_WIN_H - 1 - d] for d in range(NA_WIN_H)], axis=1)
    b = jnp.where(col_in[None, None, None], b * LOG2E, NEG_BIG)
    b = b.transpose(0, 1, 3, 2, 4).reshape(NA_HEADS, NA_WIN_H, GRID_W, NA_KEYS)
    b = b.reshape(NA_HEADS // NA_GROUP, NA_GROUP, NA_WIN_H, GRID_W, NA_KEYS).transpose(0, 2, 1, 3, 4)
    return b.reshape(NA_HEADS // NA_GROUP, NA_WIN_H, NA_LANES, NA_KEYS).astype(F32)


def _block_diag(w):
    per = BD_TILE // QKV_BLOCK
    wt = w.transpose(0, 2, 1).reshape(N_BD, per, QKV_BLOCK, QKV_BLOCK)
    eye = jnp.eye(per, dtype=w.dtype)
    return jnp.einsum('caio,ab->caibo', wt, eye).reshape(N_BD, BD_TILE, BD_TILE)


def _prep(g_norm, g_final, w_in0, g_qlat, g_kvlat, w_uq, w_ukv, na_rpb, w_out0,
          w_in1, conv_w, conv_b, w_q, w_k, w_v, w_gate, b_gate, g_mh, skip, w_out1):
    p = {}
    w0 = w_in0[0]
    o = np.cumsum((Q_LORA, KV_LORA, QK_ROPE, NA_WIDTH, NA_WIDTH, NA_WIDTH))
    kr = jnp.zeros((D_MODEL, HEAD_PAD), F32).at[:, QK_NOPE:QK_NOPE + QK_ROPE].set(w0[:, o[1]:o[2]])
    p['wa'] = jnp.concatenate([w0[:, :o[1]], kr], axis=-1).astype(BF16)
    p['wb'] = w0[:, o[2]:].astype(BF16)
    wq = w_uq[0].reshape(Q_LORA, MLA_HEADS, QK_NOPE + QK_ROPE)
    p['wuq'] = jnp.pad(wq, ((0, 0), (0, 0), (0, HEAD_PAD - QK_NOPE - QK_ROPE))).reshape(Q_LORA, MLA_QK).astype(BF16)
    wkv = w_ukv[0].reshape(KV_LORA, MLA_HEADS, QK_NOPE + V_DIM)
    wk = jnp.pad(wkv[:, :, :QK_NOPE], ((0, 0), (0, 0), (0, HEAD_PAD - QK_NOPE))).reshape(KV_LORA, MLA_QK)
    wv = jnp.pad(wkv[:, :, QK_NOPE:], ((0, 0), (0, 0), (0, HEAD_PAD - V_DIM))).reshape(KV_LORA, MLA_QK)
    p['wukv'] = jnp.concatenate([wk, wv], axis=-1).astype(BF16)
    p['gq'] = g_qlat[0].reshape(1, Q_LORA)
    p['gkv'] = g_kvlat[0].reshape(1, KV_LORA)
    p['g0'] = g_norm[0].reshape(1, D_MODEL)
    p['g1'] = g_norm[1].reshape(1, D_MODEL)
    p['gf'] = g_final.reshape(1, D_MODEL)
    p['bias'] = _na_bias(na_rpb[0])
    p['wo0'] = w_out0[0].astype(BF16)
    p['wi1'] = w_in1[0].astype(BF16)
    p['cw'] = conv_w[0]
    p['cb'] = conv_b[0].reshape(1, ML_INNER)
    gate_perm = np.array([d * 2 * ML_HEADS + h for h in range(ML_HEADS) for d in range(2)]
                         + [d * 2 * ML_HEADS + ML_HEADS + h for h in range(ML_HEADS) for d in range(2)])
    wg = w_gate[0][:, gate_perm].reshape(3, ML_INNER // QKV_BLOCK, QKV_BLOCK, 2 * GATE_ROWS)

    def fold(w, g):
        f = jnp.einsum('noi,nog->nig', w, g, precision=lax.Precision.HIGHEST).reshape(ML_INNER, 2 * GATE_ROWS)
        return jnp.pad(f, ((0, 0), (0, LANES - 2 * GATE_ROWS))).reshape(N_BD, BD_TILE, LANES)

    p['bdqk'] = jnp.concatenate([_block_diag(w_q[0]), _block_diag(w_k[0]),
                                 fold(w_q[0], wg[0]) + fold(w_k[0], wg[1])], axis=-1).astype(BF16)
    p['bdv'] = jnp.concatenate([_block_diag(w_v[0]), fold(w_v[0], wg[2])], axis=-1).astype(BF16)
    p['bg'] = jnp.pad(b_gate[0][gate_perm].reshape(1, -1), ((0, 0), (0, LANES - 2 * GATE_ROWS)))
    p['gmh'] = g_mh[0].reshape(1, ML_INNER)
    p['skip'] = skip[0].reshape(1, ML_INNER)
    p['wo1'] = w_out1[0].astype(BF16)
    return p


def _trunk(x, mod0, mod1, p, tabs, tm=512):
    B, S, _ = x.shape
    rc, rm, rp = tabs
    q, k, v, nq, nk, nv, gate = _in0(x, mod0, p['g0'], p['wa'], p['wb'], p['gq'], p['gkv'],
                                      p['wuq'], p['wukv'], rc, rm, rp, tm=tm)
    o_mla = _mla(q, k, v)
    o_na = _natten(nq, nk, nv, p['bias'])
    x1, xm, z = _mid(o_mla, o_na, gate, x, mod0, mod1, p['g1'], p['wo0'], p['wi1'], tm=tm)
    mq, mk, mv, xc, pre = _conv(xm, p['cw'], p['cb'], p['bdqk'], p['bdv'], p['bg'], tm=tm)
    gc, gr = _gates(pre)
    hn = _mlstm(mq, mk, mv, gc, gr, p['gmh'])
    return _out(hn, xc, z, x1, mod1, p['skip'], p['wo1'], p['gf'], tm=tm)


def kernel(x_prompt, x_sample, c_prompt, c_sample, g_norm, w_ada, b_ada, g_final, w_in0, g_qlat, g_kvlat, w_uq, w_ukv, na_rpb, w_out0, w_in1, conv_w, conv_b, w_q, w_k, w_v, w_gate, b_gate, g_mh, skip, w_out1):
    p = _prep(g_norm, g_final, w_in0, g_qlat, g_kvlat, w_uq, w_ukv, na_rpb, w_out0,
              w_in1, conv_w, conv_b, w_q, w_k, w_v, w_gate, b_gate, g_mh, skip, w_out1)
    nb_p = x_prompt.shape[0]
    mod = _ada(jnp.concatenate([c_prompt, c_sample], axis=0), w_ada, b_ada)
    mod = mod.reshape(DEPTH, -1, 3, D_MODEL)
    outs = []
    for x, sl in ((x_prompt, slice(0, nb_p)), (x_sample, slice(nb_p, None))):
        tabs = _rope_tables(x.shape[1])
        outs.append(_trunk(x, mod[0, sl], mod[1, sl], p, tabs))
    return tuple(outs)
```

```python
import functools

import numpy as np
import jax
import jax.numpy as jnp
from jax import lax
from jax.experimental import pallas as pl
from jax.experimental.pallas import tpu as pltpu

F32 = jnp.float32
BF16 = jnp.bfloat16

D_MODEL = 1024
DEPTH = 2
GRID_W = 64
EPS = 1e-6
MLA_HEADS = 8
QK_NOPE = 64
QK_ROPE = 32
V_DIM = 64
Q_LORA = 256
KV_LORA = 256
ROPE_THETA = 10000.0
MLA_WIDTH = MLA_HEADS * V_DIM
NA_HEADS = 8
NA_DIM = 64
NA_WIN_H = 8
NA_WIN_W = 16
NA_WIDTH = NA_HEADS * NA_DIM
ML_HEADS = 4
ML_INNER = 2 * D_MODEL
ML_DIM = ML_INNER // ML_HEADS
ML_CONV = 5
QKV_BLOCK = 4
ML_L = 256

HEAD_PAD = 128
MLA_QK = MLA_HEADS * HEAD_PAD
ROPE_HALF = QK_ROPE // 2
LANES = 128
BD_TILE = 256
N_BD = ML_INNER // BD_TILE
NEG_BIG = -1e30
LOG2E = 1.4426950408889634

VMEM_LIMIT = 56 * 1024 * 1024

_NT = (((1,), (1,)), ((), ()))
_TN = (((0,), (0,)), ((), ()))


def _cparams(sem):
    return pltpu.CompilerParams(dimension_semantics=sem, vmem_limit_bytes=VMEM_LIMIT)


def _silu(x):
    return x * jax.nn.sigmoid(x)


def _rms(x, g):
    return x * lax.rsqrt(jnp.mean(x * x, axis=-1, keepdims=True) + EPS) * g


def _ada_kernel(c_ref, w_ref, b_ref, o_ref):
    c = c_ref[...]
    o_ref[0] = jnp.dot(_silu(c).astype(BF16), w_ref[0].astype(BF16), preferred_element_type=F32) + b_ref[0]


def _ada(c_all, w_ada, b_ada):
    nb = c_all.shape[0]
    tn = 512
    return pl.pallas_call(
        _ada_kernel,
        grid=(DEPTH, 3 * D_MODEL // tn),
        in_specs=[pl.BlockSpec((nb, D_MODEL), lambda l, j: (0, 0)),
                  pl.BlockSpec((1, D_MODEL, tn), lambda l, j: (l, 0, j)),
                  pl.BlockSpec((1, 1, tn), lambda l, j: (l, 0, j))],
        out_specs=pl.BlockSpec((1, nb, tn), lambda l, j: (l, 0, j)),
        out_shape=jax.ShapeDtypeStruct((DEPTH, nb, 3 * D_MODEL), F32),
        compiler_params=_cparams(("arbitrary", "arbitrary")),
        name="ada",
    )(c_all, w_ada, b_ada.reshape(DEPTH, 1, 3 * D_MODEL))


def _in0_kernel(x_ref, mod_ref, g_ref, wa_ref, wb_ref, gq_ref, gkv_ref, wuq_ref, wukv_ref,
                rc_ref, rm_ref, rp_ref,
                q_ref, k_ref, v_ref, nq_ref, nk_ref, nv_ref, gate_ref):
    x = x_ref[0]
    shift = mod_ref[0, 0:1, :]
    scale = mod_ref[0, 1:2, :]
    h = _rms(x, g_ref[...]) * (1.0 + scale) + shift
    hb = h.astype(BF16)

    ua = jnp.dot(hb, wa_ref[...], preferred_element_type=F32)
    ub = jnp.dot(hb, wb_ref[...], preferred_element_type=F32)
    qn = _rms(ua[:, :Q_LORA], gq_ref[...]).astype(BF16)
    kvn = _rms(ua[:, Q_LORA:Q_LORA + KV_LORA], gkv_ref[...]).astype(BF16)
    kr = ua[:, Q_LORA + KV_LORA:]
    qf = jnp.dot(qn, wuq_ref[...], preferred_element_type=F32)
    kvf = jnp.dot(kvn, wukv_ref[...], preferred_element_type=F32)

    rc, rm, rp = rc_ref[...], rm_ref[...], rp_ref[...]

    def rope(t):
        return (t * rc + pltpu.roll(t, LANES - ROPE_HALF, 1) * rm + pltpu.roll(t, ROPE_HALF, 1) * rp)

    krr = rope(kr)
    qscale = (QK_NOPE + QK_ROPE) ** -0.5 * LOG2E
    for hh in range(MLA_HEADS):
        sl = slice(hh * HEAD_PAD, (hh + 1) * HEAD_PAD)
        q_ref[0, :, sl] = (rope(qf[:, sl]) * qscale).astype(BF16)
        k_ref[0, :, sl] = (kvf[:, sl] + krr).astype(BF16)
    ones_pad = (lax.broadcasted_iota(jnp.int32, (1, MLA_QK), 1) % HEAD_PAD >= V_DIM).astype(F32)
    v_ref[0] = (kvf[:, MLA_QK:] + ones_pad).astype(BF16)

    nq_ref[0] = (ub[:, :NA_WIDTH] * (NA_DIM ** -0.5 * LOG2E)).astype(BF16)
    nk_ref[0] = ub[:, NA_WIDTH:2 * NA_WIDTH].astype(BF16)
    nv_ref[0] = ub[:, 2 * NA_WIDTH:3 * NA_WIDTH].astype(BF16)
    gate_ref[0] = ub[:, 3 * NA_WIDTH:].astype(BF16)


def _in0(x, mod, g, wa, wb, gq, gkv, wuq, wukv, rc, rm, rp, tm=512):
    B, S, D = x.shape
    full = lambda a: pl.BlockSpec(a.shape, lambda b, i: (0,) * a.ndim)
    tok = lambda w: pl.BlockSpec((1, tm, w), lambda b, i: (b, i, 0))
    tab = pl.BlockSpec((tm, LANES), lambda b, i: (i, 0))
    widths = (MLA_QK, MLA_QK, MLA_QK, NA_WIDTH, NA_WIDTH, NA_WIDTH, MLA_WIDTH + NA_WIDTH)
    return pl.pallas_call(
        _in0_kernel,
        grid=(B, S // tm),
        in_specs=[tok(D), pl.BlockSpec((1, 3, D), lambda b, i: (b, 0, 0)), full(g), full(wa), full(wb),
                  full(gq), full(gkv), full(wuq), full(wukv), tab, tab, tab],
        out_specs=[tok(w) for w in widths],
        out_shape=[jax.ShapeDtypeStruct((B, S, w), BF16) for w in widths],
        compiler_params=_cparams(("parallel", "parallel")),
        name="in0",
    )(x, mod, g, wa, wb, gq, gkv, wuq, wukv, rc, rm, rp)


def _mla_kernel(q_ref, k_ref, v_ref, o_ref):
    low = lax.broadcasted_iota(jnp.int32, (q_ref.shape[1], HEAD_PAD), 1) < V_DIM

    def scores(hh):
        sl = slice(hh * HEAD_PAD, (hh + 1) * HEAD_PAD)
        return lax.dot_general(q_ref[0, :, sl], k_ref[0, :, sl], _NT, preferred_element_type=F32)

    s_next = scores(0)
    outs = []
    for hh in range(MLA_HEADS):
        s = s_next
        if hh + 1 < MLA_HEADS:
            s_next = scores(hh + 1)
        sl = slice(hh * HEAD_PAD, (hh + 1) * HEAD_PAD)
        p = jnp.exp2(s - jnp.max(s, axis=-1, keepdims=True))
        ol = jnp.dot(p.astype(BF16), v_ref[0, :, sl], preferred_element_type=F32)
        outs.append(ol / jnp.where(low, pltpu.roll(ol, V_DIM, 1), 1.0))
        if hh % 2 == 1:
            vs = slice((hh - 1) * V_DIM, (hh + 1) * V_DIM)
            o_ref[0, :, vs] = jnp.where(low, outs[hh - 1], pltpu.roll(outs[hh], V_DIM, 1)).astype(BF16)


def _mla(q, k, v, tq=512):
    B, S, _ = q.shape
    return pl.pallas_call(
        _mla_kernel,
        grid=(B, S // tq),
        in_specs=[pl.BlockSpec((1, tq, MLA_QK), lambda b, i: (b, i, 0)),
                  pl.BlockSpec((1, S, MLA_QK), lambda b, i: (b, 0, 0)),
                  pl.BlockSpec((1, S, MLA_QK), lambda b, i: (b, 0, 0))],
        out_specs=pl.BlockSpec((1, tq, MLA_WIDTH), lambda b, i: (b, i, 0)),
        out_shape=jax.ShapeDtypeStruct((B, S, MLA_WIDTH), BF16),
        compiler_params=_cparams(("parallel", "arbitrary")),
        name="mla",
    )(q, k, v)


NA_GROUP = 4
NA_LANES = NA_GROUP * NA_DIM
NA_KEYS = NA_WIN_H * GRID_W
NA_UNROLL = 8


def _na_kernel(q_ref, k_ref, v_ref, bias_ref, o_ref, *, rows):
    rh = lax.broadcasted_iota(jnp.int32, (NA_LANES, NA_LANES), 0) // GRID_W
    lh = lax.broadcasted_iota(jnp.int32, (NA_LANES, NA_LANES), 1) // NA_DIM
    hmask = (rh == lh).astype(F32)

    def window(r):
        r0 = jnp.clip(r - NA_WIN_H // 2, 0, rows - NA_WIN_H)
        return r - r0, pl.ds(pl.multiple_of(r0 * GRID_W, GRID_W), NA_KEYS)

    def scores(r):
        d, win = window(r)
        qr = q_ref[0, pl.ds(pl.multiple_of(r * GRID_W, GRID_W), GRID_W), :].astype(F32)
        qs = (jnp.concatenate([qr] * NA_GROUP, axis=0) * hmask).astype(BF16)
        return lax.dot_general(qs, k_ref[0, win, :], _NT, preferred_element_type=F32) + bias_ref[0, d]

    def body(t, carry):
        base = t * NA_UNROLL
        s_next = scores(base)
        for i in range(NA_UNROLL):
            r = base + i
            s = s_next
            if i + 1 < NA_UNROLL:
                s_next = scores(r + 1)
            _, win = window(r)
            p = jnp.exp2(s - jnp.max(s, axis=-1, keepdims=True))
            l = jnp.sum(p, axis=-1, keepdims=True)
            o = jnp.dot(p.astype(BF16), v_ref[0, win, :], preferred_element_type=F32) / l * hmask
            out = o[0:GRID_W]
            for g in range(1, NA_GROUP):
                out = out + o[g * GRID_W:(g + 1) * GRID_W]
            o_ref[0, pl.ds(pl.multiple_of(r * GRID_W, GRID_W), GRID_W), :] = out.astype(BF16)
        return carry

    lax.fori_loop(0, rows // NA_UNROLL, body, 0)


def _natten(q, k, v, bias):
    B, S, _ = q.shape
    rows = S // GRID_W
    tok = pl.BlockSpec((1, S, NA_LANES), lambda g, b: (b, 0, g))
    return pl.pallas_call(
        functools.partial(_na_kernel, rows=rows),
        grid=(NA_HEADS // NA_GROUP, B),
        in_specs=[tok, tok, tok,
                  pl.BlockSpec((1, NA_WIN_H, NA_LANES, NA_KEYS), lambda g, b: (g, 0, 0, 0))],
        out_specs=tok,
        out_shape=jax.ShapeDtypeStruct((B, S, NA_WIDTH), BF16),
        compiler_params=_cparams(("parallel", "parallel")),
        name="natten",
    )(q, k, v, bias)


def _mid_kernel(om_ref, on_ref, gate_ref, x_ref, mod0_ref, mod1_ref, g_ref, wo_ref, wi_ref,
                x1_ref, xm_ref, z_ref):
    tm = x_ref.shape[1]
    halves = [slice(0, tm // 2), slice(tm // 2, tm)]

    def out_proj(rs):
        sg = _silu(gate_ref[0, rs, :].astype(F32))
        o = jnp.concatenate([om_ref[0, rs, :].astype(F32), on_ref[0, rs, :].astype(F32)], axis=-1) * sg
        return jnp.dot(o.astype(BF16), wo_ref[...], preferred_element_type=F32)

    outs = [out_proj(rs) for rs in halves]
    for rs, out in zip(halves, outs):
        x1 = x_ref[0, rs, :] + mod0_ref[0, 2:3, :] * out
        x1_ref[0, rs, :] = x1
        h = _rms(x1, g_ref[...]) * (1.0 + mod1_ref[0, 1:2, :]) + mod1_ref[0, 0:1, :]
        u = jnp.dot(h.astype(BF16), wi_ref[...], preferred_element_type=F32)
        xm_ref[0, rs, :] = u[:, :ML_INNER].astype(BF16)
        z_ref[0, rs, :] = u[:, ML_INNER:].astype(BF16)


def _mid(om, on, gate, x, mod0, mod1, g, wo, wi, tm=512):
    B, S, D = x.shape
    full = lambda a: pl.BlockSpec(a.shape, lambda b, i: (0,) * a.ndim)
    tok = lambda w: pl.BlockSpec((1, tm, w), lambda b, i: (b, i, 0))
    modspec = pl.BlockSpec((1, 3, D), lambda b, i: (b, 0, 0))
    return pl.pallas_call(
        _mid_kernel,
        grid=(B, S // tm),
        in_specs=[tok(MLA_WIDTH), tok(NA_WIDTH), tok(D), tok(D), modspec, modspec, full(g), full(wo), full(wi)],
        out_specs=[tok(D), tok(ML_INNER), tok(ML_INNER)],
        out_shape=[jax.ShapeDtypeStruct((B, S, D), F32),
                   jax.ShapeDtypeStruct((B, S, ML_INNER), BF16),
                   jax.ShapeDtypeStruct((B, S, ML_INNER), BF16)],
        compiler_params=_cparams(("parallel", "parallel")),
        name="mid",
    )(om, on, gate, x, mod0, mod1, g, wo, wi)


HALO = 16
SHIFT_ROWS = 128
SHIFT_WIN = SHIFT_ROWS + 2 * HALO
GATE_ROWS = 2 * ML_HEADS
PADR = 8
MXU_TAPS = (ML_CONV // 2 - 1, ML_CONV // 2 + 1)
VALU_TAPS = tuple(j for j in range(ML_CONV) if j not in MXU_TAPS)


def _log_sigmoid(x):
    return jnp.minimum(x, 0.0) - jnp.log1p(jnp.exp(-jnp.abs(x)))


def _chunk_scan(x, pos, op, ident, reverse):
    n = x.shape[0]
    k = 1
    while k < ML_L:
        if reverse:
            x = op(x, jnp.where(pos < ML_L - k, pltpu.roll(x, n - k, 0), ident))
        else:
            x = op(x, jnp.where(pos >= k, pltpu.roll(x, k, 0), ident))
        k *= 2
    return x


def _conv_kernel(xm_ref, xp_ref, xn_ref, sh_ref, cw_ref, cb_ref, bdqk_ref, bdv_ref, bg_ref,
                 q_ref, k_ref, v_ref, xc_ref, pre_ref, ext_ref, *, tm):
    i = pl.program_id(1)
    n = pl.num_programs(1)
    ext_ref[0:PADR, :] = jnp.where(i > 0, xp_ref[0].astype(F32)[HALO - PADR:, :], 0.0)
    ext_ref[PADR:PADR + tm, :] = xm_ref[0].astype(F32)
    ext_ref[PADR + tm:, :] = jnp.where(i < n - 1, xn_ref[0].astype(F32)[:PADR, :], 0.0)
    zero_halo = jnp.zeros((HALO, BD_TILE), BF16)
    qscale = ML_DIM ** -0.5
    pre = jnp.zeros((tm, LANES), F32) + bg_ref[...]

    def shifted(c):
        cs = slice(c * BD_TILE, (c + 1) * BD_TILE)
        prev = jnp.where(i > 0, xp_ref[0, :, cs], zero_halo)
        nxt = jnp.where(i < n - 1, xn_ref[0, :, cs], zero_halo)
        out = []
        for rb in range(tm // SHIFT_ROWS):
            lo, hi = rb * SHIFT_ROWS - HALO, (rb + 1) * SHIFT_ROWS + HALO
            parts = [prev] if lo < 0 else []
            parts.append(xm_ref[0, max(lo, 0):min(hi, tm), cs])
            if hi > tm:
                parts.append(nxt)
            win = jnp.concatenate(parts, axis=0) if len(parts) > 1 else parts[0]
            out.append(jnp.dot(sh_ref[...], win, preferred_element_type=F32))
        return out

    sh_next = shifted(0)
    for c in range(N_BD):
        cs = slice(c * BD_TILE, (c + 1) * BD_TILE)
        sh_cur = sh_next
        ys = []
        for rb in range(tm // SHIFT_ROWS):
            y = cb_ref[:, cs]
            for t, j in enumerate(MXU_TAPS):
                y = y + sh_cur[rb][t * SHIFT_ROWS:(t + 1) * SHIFT_ROWS] * cw_ref[j:j + 1, cs]
            for j in VALU_TAPS:
                r0 = PADR + rb * SHIFT_ROWS + j - ML_CONV // 2
                y = y + ext_ref[r0:r0 + SHIFT_ROWS, cs] * cw_ref[j:j + 1, cs]
            ys.append(y)
        xc = _silu(jnp.concatenate(ys, axis=0))
        xcb = xc.astype(BF16)
        xc_ref[0, :, cs] = xcb
        if c + 1 < N_BD:
            sh_next = shifted(c + 1)
        qk = jnp.dot(xcb, bdqk_ref[c], preferred_element_type=F32)
        vv = jnp.dot(xm_ref[0, :, cs], bdv_ref[c], preferred_element_type=F32)
        pre = pre + qk[:, 2 * BD_TILE:] + vv[:, BD_TILE:]
        q_ref[0, :, cs] = (qk[:, :BD_TILE] * qscale).astype(BF16)
        k_ref[0, :, cs] = qk[:, BD_TILE:2 * BD_TILE].astype(BF16)
        v_ref[0, :, cs] = vv[:, :BD_TILE].astype(BF16)

    pre_ref[0] = pre


def _gates_kernel(pre_ref, gc_ref, gr_ref):
    pre = pre_ref[0]
    lane = lax.broadcasted_iota(jnp.int32, pre.shape, 1)
    pos = lax.broadcasted_iota(jnp.int32, pre.shape, 0) % ML_L
    fwd = lane % 2 == 0
    lf = _log_sigmoid(pre)
    b = jnp.where(fwd, _chunk_scan(lf, pos, jnp.add, 0.0, False), _chunk_scan(lf, pos, jnp.add, 0.0, True))
    a = pre - pltpu.roll(b, LANES - GATE_ROWS, 1)
    c = jnp.where(fwd, _chunk_scan(a, pos, jnp.maximum, -jnp.inf, False),
                  _chunk_scan(a, pos, jnp.maximum, -jnp.inf, True))
    g = jnp.where(lane < GATE_ROWS, a, jnp.where(lane < 2 * GATE_ROWS, b, pltpu.roll(c, 2 * GATE_ROWS, 1)))
    gc_ref[0] = g
    gr_ref[0] = g.T[:3 * GATE_ROWS, :]


def _shift_matrix():
    m = np.zeros((len(MXU_TAPS) * SHIFT_ROWS, SHIFT_WIN), np.float32)
    r = np.arange(SHIFT_ROWS)
    for t, j in enumerate(MXU_TAPS):
        m[t * SHIFT_ROWS + r, r + HALO + j - ML_CONV // 2] = 1.0
    return jnp.asarray(m, dtype=BF16)


def _conv(xm, cw, cb, bdqk, bdv, bg, tm=512):
    sh = _shift_matrix()
    B, S, _ = xm.shape
    hb = tm // HALO
    full = lambda a: pl.BlockSpec(a.shape, lambda b, i: (0,) * a.ndim)
    tok = lambda w: pl.BlockSpec((1, tm, w), lambda b, i: (b, i, 0))
    return pl.pallas_call(
        functools.partial(_conv_kernel, tm=tm),
        grid=(B, S // tm),
        in_specs=[tok(ML_INNER),
                  pl.BlockSpec((1, HALO, ML_INNER), lambda b, i: (b, jnp.maximum(i * hb - 1, 0), 0)),
                  pl.BlockSpec((1, HALO, ML_INNER), lambda b, i: (b, jnp.minimum((i + 1) * hb, S // HALO - 1), 0)),
                  full(sh), full(cw), full(cb), full(bdqk), full(bdv), full(bg)],
        out_specs=[tok(ML_INNER), tok(ML_INNER), tok(ML_INNER), tok(ML_INNER), tok(LANES)],
        out_shape=[jax.ShapeDtypeStruct((B, S, ML_INNER), BF16)] * 4 + [jax.ShapeDtypeStruct((B, S, LANES), F32)],
        scratch_shapes=[pltpu.VMEM((tm + 2 * PADR, ML_INNER), F32)],
        compiler_params=_cparams(("parallel", "parallel")),
        name="conv",
    )(xm, xm, xm, sh, cw, cb, bdqk, bdv, bg)


def _gates(pre):
    B, S, _ = pre.shape
    return pl.pallas_call(
        _gates_kernel,
        grid=(B,),
        in_specs=[pl.BlockSpec((1, S, LANES), lambda b: (b, 0, 0))],
        out_specs=[pl.BlockSpec((1, S, LANES), lambda b: (b, 0, 0)),
                   pl.BlockSpec((1, 3 * GATE_ROWS, S), lambda b: (b, 0, 0))],
        out_shape=[jax.ShapeDtypeStruct((B, S, LANES), F32), jax.ShapeDtypeStruct((B, 3 * GATE_ROWS, S), F32)],
        compiler_params=_cparams(("parallel",)),
        name="gates",
    )(pre)


ML_UNROLL = 4
OUT_UNROLL = 4


def _mlstm_kernel(q_ref, k_ref, v_ref, gc_ref, gr_ref, gmh_ref, o_ref,
                  cf_ref, cb_ref, sf_ref, sb_ref, nst_ref, mf_ref, mb_ref, mrep_ref, brep_ref, *, seq):
    L = ML_L
    nc = seq // L
    head = pl.program_id(1)
    cf_ref[...] = jnp.zeros_like(cf_ref)
    cb_ref[...] = jnp.zeros_like(cb_ref)
    sf_ref[0] = jnp.zeros(sf_ref.shape[1:], BF16)
    sb_ref[nc] = jnp.zeros(sb_ref.shape[1:], BF16)
    nst_ref[...] = jnp.zeros_like(nst_ref)
    lane = lax.broadcasted_iota(jnp.int32, (L, LANES), 1)

    def wide(x, n):
        return jnp.concatenate([x] * n, axis=1)

    def state_step(c, d, ct_ref, st_ref, mst_ref, nvec, m):
        rows = pl.ds(pl.multiple_of(c * L, L), L)
        r = head * 2 + d
        kc, vc = k_ref[0, rows, :], v_ref[0, rows, :]
        gcol = gc_ref[0, rows, :]
        a_col, b_col, c_col = [jnp.sum(jnp.where(lane == g * GATE_ROWS + r, gcol, 0.0), axis=1, keepdims=True)
                               for g in range(3)]
        a_row = gr_ref[0, pl.ds(r, 1), rows]
        nst_ref[c, d:d + 1, :] = nvec
        mst_ref[c] = jnp.broadcast_to(m, mst_ref.shape[1:])
        mrep = jnp.maximum(c_col, m)
        mrep_ref[d, rows, :] = jnp.broadcast_to(mrep, (L, LANES))
        brep_ref[d, rows, :] = jnp.broadcast_to(b_col + mrep, (L, LANES))
        end = L - 1 if d == 0 else 0
        b_last, c_last = b_col[end:end + 1, :], c_col[end:end + 1, :]
        m_new = b_last + jnp.maximum(m, c_last)
        w_old = jnp.exp(b_last + m - m_new)
        w_s = jnp.exp(b_last + a_col - m_new)
        w_row = jnp.exp(b_last + a_row - m_new)
        wv = (w_s * vc.astype(F32)).astype(BF16)
        nxt = c + 1 if d == 0 else c
        for rb in range(ML_DIM // LANES):
            rs = slice(rb * LANES, (rb + 1) * LANES)
            blk = w_old * ct_ref[rs, :] + lax.dot_general(kc[:, rs], wv, _TN, preferred_element_type=F32)
            ct_ref[rs, :] = blk
            st_ref[nxt, rs, :] = blk.astype(BF16)
        dn = jnp.dot(jnp.broadcast_to(w_row, (8, L)).astype(BF16), kc, preferred_element_type=F32)
        return w_old * nvec + dn[0:1], m_new

    def scan_body(j, carry):
        nf, mf, nb, mb = carry
        nf, mf = state_step(j, 0, cf_ref, sf_ref, mf_ref, nf, mf)
        nb, mb = state_step(nc - 1 - j, 1, cb_ref, sb_ref, mb_ref, nb, mb)
        return nf, mf, nb, mb

    z_n = jnp.zeros((1, ML_DIM), F32)
    z_m = jnp.zeros((1, 1), F32)
    lax.fori_loop(0, nc, scan_body, (z_n, z_m, z_n, z_m), unroll=ML_UNROLL)

    ri = lax.broadcasted_iota(jnp.int32, (L, L), 0)
    ci = lax.broadcasted_iota(jnp.int32, (L, L), 1)

    def decay(d, rows, m, mask):
        a_row = gr_ref[0, pl.ds(head * 2 + d, 1), rows]
        mrep = mrep_ref[d, rows, :]
        dmat = jnp.exp(jnp.where(mask, a_row - wide(mrep, L // LANES), -jnp.inf))
        return dmat, jnp.exp(m - mrep), jnp.exp(-brep_ref[d, rows, :])

    def front(c):
        rows = pl.ds(pl.multiple_of(c * L, L), L)
        qc = q_ref[0, rows, :]
        kn = jnp.concatenate([k_ref[0, rows, :], nst_ref[c].astype(BF16)], axis=0)
        sx = lax.dot_general(qc, kn, _NT, preferred_element_type=F32)
        hf = jnp.dot(qc, sf_ref[c], preferred_element_type=F32)
        hb = jnp.dot(qc, sb_ref[c + 1], preferred_element_type=F32)
        return rows, sx, hf, hb

    def finish(c, rows, sx, hf, hb):
        d_f, wi_f, e_f = decay(0, rows, mf_ref[c][0:1, 0:1], ci <= ri)
        d_b, wi_b, e_b = decay(1, rows, mb_ref[c][0:1, 0:1], ci >= ri)
        s = sx[:, :L]
        s_f, s_b = s * d_f, s * d_b
        qn_f = jnp.broadcast_to(sx[:, L:L + 1], (L, LANES))
        qn_b = jnp.broadcast_to(sx[:, L + 1:L + 2], (L, LANES))
        r_f = 1.0 / jnp.maximum(jnp.abs(wi_f * qn_f + jnp.sum(s_f, axis=-1, keepdims=True)), e_f)
        r_b = 1.0 / jnp.maximum(jnp.abs(wi_b * qn_b + jnp.sum(s_b, axis=-1, keepdims=True)), e_b)
        p = (s_f * wide(r_f, L // LANES) + s_b * wide(r_b, L // LANES)).astype(BF16)
        hs = (hf * wide(wi_f * r_f, ML_DIM // LANES) + hb * wide(wi_b * r_b, ML_DIM // LANES)
              + jnp.dot(p, v_ref[0, rows, :], preferred_element_type=F32))
        mu = jnp.mean(hs, axis=-1, keepdims=True)
        var = jnp.mean(jnp.square(hs - mu), axis=-1, keepdims=True)
        o_ref[0, rows, :] = ((hs - mu) * lax.rsqrt(var + EPS) * gmh_ref[...]).astype(BF16)

    def out_body(t, carry):
        base = t * OUT_UNROLL
        nxt = front(base)
        for i in range(OUT_UNROLL):
            cur = nxt
            if i + 1 < OUT_UNROLL:
                nxt = front(base + i + 1)
            finish(base + i, *cur)
        return carry

    lax.fori_loop(0, nc // OUT_UNROLL, out_body, 0)


def _mlstm(q, k, v, gc, gr, gmh):
    B, S, _ = q.shape
    nc = S // ML_L
    tok = pl.BlockSpec((1, S, ML_DIM), lambda b, h: (b, 0, h))
    return pl.pallas_call(
        functools.partial(_mlstm_kernel, seq=S),
        grid=(B, ML_HEADS),
        in_specs=[tok, tok, tok,
                  pl.BlockSpec((1, S, LANES), lambda b, h: (b, 0, 0)),
                  pl.BlockSpec((1, 3 * GATE_ROWS, S), lambda b, h: (b, 0, 0)),
                  pl.BlockSpec((1, ML_DIM), lambda b, h: (0, h))],
        out_specs=tok,
        out_shape=jax.ShapeDtypeStruct((B, S, ML_INNER), BF16),
        scratch_shapes=[pltpu.VMEM((ML_DIM, ML_DIM), F32), pltpu.VMEM((ML_DIM, ML_DIM), F32),
                        pltpu.VMEM((nc + 1, ML_DIM, ML_DIM), BF16), pltpu.VMEM((nc + 1, ML_DIM, ML_DIM), BF16),
                        pltpu.VMEM((nc, 16, ML_DIM), F32),
                        pltpu.VMEM((nc, 8, LANES), F32), pltpu.VMEM((nc, 8, LANES), F32),
                        pltpu.VMEM((2, S, LANES), F32), pltpu.VMEM((2, S, LANES), F32)],
        compiler_params=_cparams(("parallel", "arbitrary")),
        name="mlstm",
    )(q, k, v, gc, gr, gmh)


def _out_kernel(hn_ref, xc_ref, z_ref, x1_ref, mod1_ref, skip_ref, wo_ref, gf_ref, y_ref):
    o = (hn_ref[0].astype(F32) + skip_ref[...] * xc_ref[0].astype(F32)) * _silu(z_ref[0].astype(F32))
    out = jnp.dot(o.astype(BF16), wo_ref[...], preferred_element_type=F32)
    x2 = x1_ref[0] + mod1_ref[0, 2:3, :] * out
    y_ref[0] = _rms(x2, gf_ref[...])


def _out(hn, xc, z, x1, mod1, skip, wo, gf, tm=512):
    B, S, D = x1.shape
    full = lambda a: pl.BlockSpec(a.shape, lambda b, i: (0,) * a.ndim)
    tok = lambda w: pl.BlockSpec((1, tm, w), lambda b, i: (b, i, 0))
    return pl.pallas_call(
        _out_kernel,
        grid=(B, S // tm),
        in_specs=[tok(ML_INNER), tok(ML_INNER), tok(ML_INNER), tok(D),
                  pl.BlockSpec((1, 3, D), lambda b, i: (b, 0, 0)), full(skip), full(wo), full(gf)],
        out_specs=tok(D),
        out_shape=jax.ShapeDtypeStruct((B, S, D), F32),
        compiler_params=_cparams(("parallel", "parallel")),
        name="out",
    )(hn, xc, z, x1, mod1, skip, wo, gf)


def _rope_tables(S):
    pos = jnp.arange(S, dtype=F32)
    inv = 1.0 / (ROPE_THETA ** (jnp.arange(0, QK_ROPE, 2, dtype=F32) / QK_ROPE))
    ang = pos[:, None] * inv[None, :]
    cos, sin = jnp.cos(ang), jnp.sin(ang)
    ones = jnp.ones((S, QK_NOPE), F32)
    z16 = jnp.zeros((S, ROPE_HALF), F32)
    z32 = jnp.zeros((S, HEAD_PAD - QK_NOPE - QK_ROPE), F32)
    z64 = jnp.zeros((S, QK_NOPE), F32)
    rc = jnp.concatenate([ones, cos, cos, z32], axis=-1)
    rm = jnp.concatenate([z64, -sin, z16, z32], axis=-1)
    rp = jnp.concatenate([z64, z16, sin, z32], axis=-1)
    return rc, rm, rp


def _na_bias(rpb):
    col = np.arange(GRID_W)
    col_start = np.clip(col - NA_WIN_W // 2, 0, GRID_W - NA_WIN_W)
    col_in = (col[None, :] >= col_start[:, None]) & (col[None, :] < col_start[:, None] + NA_WIN_W)
    period = 2 * GRID_W
    n_r = 2 * NA_WIN_H - 1
    vec = jnp.zeros((NA_HEADS, n_r, period), F32)
    vec = vec.at[..., :NA_WIN_W].set(rpb[..., NA_WIN_W - 1:])
    vec = vec.at[..., period - (NA_WIN_W - 1):].set(rpb[..., :NA_WIN_W - 1])
    toep = jnp.tile(vec, (1, 1, GRID_W))[..., :GRID_W * (period - 1)]
    toep = toep.reshape(NA_HEADS, n_r, GRID_W, period - 1)[..., :GRID_W]
    b = jnp.stack([toep[:, NA_WIN_H - 1 - d:2 * NA_WIN_H - 1 - d] for d in range(NA_WIN_H)], axis=1)
    b = jnp.where(col_in[None, None, None], b * LOG2E, NEG_BIG)
    b = b.transpose(0, 1, 3, 2, 4).reshape(NA_HEADS, NA_WIN_H, GRID_W, NA_KEYS)
    b = b.reshape(NA_HEADS // NA_GROUP, NA_GROUP, NA_WIN_H, GRID_W, NA_KEYS).transpose(0, 2, 1, 3, 4)
    return b.reshape(NA_HEADS // NA_GROUP, NA_WIN_H, NA_LANES, NA_KEYS).astype(F32)


def _block_diag(w):
    per = BD_TILE // QKV_BLOCK
    wt = w.transpose(0, 2, 1).reshape(N_BD, per, QKV_BLOCK, QKV_BLOCK)
    eye = jnp.eye(per, dtype=w.dtype)
    return jnp.einsum('caio,ab->caibo', wt, eye).reshape(N_BD, BD_TILE, BD_TILE)


def _prep(g_norm, g_final, w_in0, g_qlat, g_kvlat, w_uq, w_ukv, na_rpb, w_out0,
          w_in1, conv_w, conv_b, w_q, w_k, w_v, w_gate, b_gate, g_mh, skip, w_out1):
    p = {}
    w0 = w_in0[0]
    o = np.cumsum((Q_LORA, KV_LORA, QK_ROPE, NA_WIDTH, NA_WIDTH, NA_WIDTH))
    kr = jnp.zeros((D_MODEL, HEAD_PAD), F32).at[:, QK_NOPE:QK_NOPE + QK_ROPE].set(w0[:, o[1]:o[2]])
    p['wa'] = jnp.concatenate([w0[:, :o[1]], kr], axis=-1).astype(BF16)
    p['wb'] = w0[:, o[2]:].astype(BF16)
    wq = w_uq[0].reshape(Q_LORA, MLA_HEADS, QK_NOPE + QK_ROPE)
    p['wuq'] = jnp.pad(wq, ((0, 0), (0, 0), (0, HEAD_PAD - QK_NOPE - QK_ROPE))).reshape(Q_LORA, MLA_QK).astype(BF16)
    wkv = w_ukv[0].reshape(KV_LORA, MLA_HEADS, QK_NOPE + V_DIM)
    wk = jnp.pad(wkv[:, :, :QK_NOPE], ((0, 0), (0, 0), (0, HEAD_PAD - QK_NOPE))).reshape(KV_LORA, MLA_QK)
    wv = jnp.pad(wkv[:, :, QK_NOPE:], ((0, 0), (0, 0), (0, HEAD_PAD - V_DIM))).reshape(KV_LORA, MLA_QK)
    p['wukv'] = jnp.concatenate([wk, wv], axis=-1).astype(BF16)
    p['gq'] = g_qlat[0].reshape(1, Q_LORA)
    p['gkv'] = g_kvlat[0].reshape(1, KV_LORA)
    p['g0'] = g_norm[0].reshape(1, D_MODEL)
    p['g1'] = g_norm[1].reshape(1, D_MODEL)
    p['gf'] = g_final.reshape(1, D_MODEL)
    p['bias'] = _na_bias(na_rpb[0])
    p['wo0'] = w_out0[0].astype(BF16)
    p['wi1'] = w_in1[0].astype(BF16)
    p['cw'] = conv_w[0]
    p['cb'] = conv_b[0].reshape(1, ML_INNER)
    gate_perm = np.array([d * 2 * ML_HEADS + h for h in range(ML_HEADS) for d in range(2)]
                         + [d * 2 * ML_HEADS + ML_HEADS + h for h in range(ML_HEADS) for d in range(2)])
    wg = w_gate[0][:, gate_perm].reshape(3, ML_INNER // QKV_BLOCK, QKV_BLOCK, 2 * GATE_ROWS)

    def fold(w, g):
        f = jnp.einsum('noi,nog->nig', w, g, precision=lax.Precision.HIGHEST).reshape(ML_INNER, 2 * GATE_ROWS)
        return jnp.pad(f, ((0, 0), (0, LANES - 2 * GATE_ROWS))).reshape(N_BD, BD_TILE, LANES)

    p['bdqk'] = jnp.concatenate([_block_diag(w_q[0]), _block_diag(w_k[0]),
                                 fold(w_q[0], wg[0]) + fold(w_k[0], wg[1])], axis=-1).astype(BF16)
    p['bdv'] = jnp.concatenate([_block_diag(w_v[0]), fold(w_v[0], wg[2])], axis=-1).astype(BF16)
    p['bg'] = jnp.pad(b_gate[0][gate_perm].reshape(1, -1), ((0, 0), (0, LANES - 2 * GATE_ROWS)))
    p['gmh'] = g_mh[0].reshape(1, ML_INNER)
    p['skip'] = skip[0].reshape(1, ML_INNER)
    p['wo1'] = w_out1[0].astype(BF16)
    return p


def _trunk(x, mod0, mod1, p, tabs, tm=512):
    B, S, _ = x.shape
    rc, rm, rp = tabs
    q, k, v, nq, nk, nv, gate = _in0(x, mod0, p['g0'], p['wa'], p['wb'], p['gq'], p['gkv'],
                                      p['wuq'], p['wukv'], rc, rm, rp, tm=tm)
    o_mla = _mla(q, k, v)
    o_na = _natten(nq, nk, nv, p['bias'])
    x1, xm, z = _mid(o_mla, o_na, gate, x, mod0, mod1, p['g1'], p['wo0'], p['wi1'], tm=tm)
    mq, mk, mv, xc, pre = _conv(xm, p['cw'], p['cb'], p['bdqk'], p['bdv'], p['bg'], tm=tm)
    gc, gr = _gates(pre)
    hn = _mlstm(mq, mk, mv, gc, gr, p['gmh'])
    return _out(hn, xc, z, x1, mod1, p['skip'], p['wo1'], p['gf'], tm=tm)


def kernel(x_prompt, x_sample, c_prompt, c_sample, g_norm, w_ada, b_ada, g_final, w_in0, g_qlat, g_kvlat, w_uq, w_ukv, na_rpb, w_out0, w_in1, conv_w, conv_b, w_q, w_k, w_v, w_gate, b_gate, g_mh, skip, w_out1):
    p = _prep(g_norm, g_final, w_in0, g_qlat, g_kvlat, w_uq, w_ukv, na_rpb, w_out0,
              w_in1, conv_w, conv_b, w_q, w_k, w_v, w_gate, b_gate, g_mh, skip, w_out1)
    nb_p = x_prompt.shape[0]
    mod = _ada(jnp.concatenate([c_prompt, c_sample], axis=0), w_ada, b_ada)
    mod = mod.reshape(DEPTH, -1, 3, D_MODEL)
    outs = []
    for x, sl in ((x_prompt, slice(0, nb_p)), (x_sample, slice(nb_p, None))):
        tabs = _rope_tables(x.shape[1])
        outs.append(_trunk(x, mod[0, sl], mod[1, sl], p, tabs))
    return tuple(outs)
```

```python
import functools

import numpy as np
import jax
import jax.numpy as jnp
from jax import lax
from jax.experimental import pallas as pl
from jax.experimental.pallas import tpu as pltpu

F32 = jnp.float32
BF16 = jnp.bfloat16

D_MODEL = 1024
DEPTH = 2
GRID_W = 64
EPS = 1e-6
MLA_HEADS = 8
QK_NOPE = 64
QK_ROPE = 32
V_DIM = 64
Q_LORA = 256
KV_LORA = 256
ROPE_THETA = 10000.0
MLA_WIDTH = MLA_HEADS * V_DIM
NA_HEADS = 8
NA_DIM = 64
NA_WIN_H = 8
NA_WIN_W = 16
NA_WIDTH = NA_HEADS * NA_DIM
ML_HEADS = 4
ML_INNER = 2 * D_MODEL
ML_DIM = ML_INNER // ML_HEADS
ML_CONV = 5
QKV_BLOCK = 4
ML_L = 256

HEAD_PAD = 128
MLA_QK = MLA_HEADS * HEAD_PAD
ROPE_HALF = QK_ROPE // 2
LANES = 128
BD_TILE = 256
N_BD = ML_INNER // BD_TILE
NEG_BIG = -1e30
LOG2E = 1.4426950408889634

VMEM_LIMIT = 56 * 1024 * 1024

_NT = (((1,), (1,)), ((), ()))
_TN = (((0,), (0,)), ((), ()))


def _cparams(sem):
    return pltpu.CompilerParams(dimension_semantics=sem, vmem_limit_bytes=VMEM_LIMIT)


def _silu(x):
    return x * jax.nn.sigmoid(x)


def _rms(x, g):
    return x * lax.rsqrt(jnp.mean(x * x, axis=-1, keepdims=True) + EPS) * g


def _ada_kernel(c_ref, w_ref, b_ref, o_ref):
    c = c_ref[...]
    o_ref[0] = jnp.dot(_silu(c).astype(BF16), w_ref[0].astype(BF16), preferred_element_type=F32) + b_ref[0]


def _ada(c_all, w_ada, b_ada):
    nb = c_all.shape[0]
    tn = 512
    return pl.pallas_call(
        _ada_kernel,
        grid=(DEPTH, 3 * D_MODEL // tn),
        in_specs=[pl.BlockSpec((nb, D_MODEL), lambda l, j: (0, 0)),
                  pl.BlockSpec((1, D_MODEL, tn), lambda l, j: (l, 0, j)),
                  pl.BlockSpec((1, 1, tn), lambda l, j: (l, 0, j))],
        out_specs=pl.BlockSpec((1, nb, tn), lambda l, j: (l, 0, j)),
        out_shape=jax.ShapeDtypeStruct((DEPTH, nb, 3 * D_MODEL), F32),
        compiler_params=_cparams(("arbitrary", "arbitrary")),
        name="ada",
    )(c_all, w_ada, b_ada.reshape(DEPTH, 1, 3 * D_MODEL))


def _in0_kernel(x_ref, mod_ref, g_ref, wa_ref, wb_ref, gq_ref, gkv_ref, wuq_ref, wukv_ref,
                rc_ref, rm_ref, rp_ref,
                q_ref, k_ref, v_ref, nq_ref, nk_ref, nv_ref, gate_ref):
    x = x_ref[0]
    shift = mod_ref[0, 0:1, :]
    scale = mod_ref[0, 1:2, :]
    h = _rms(x, g_ref[...]) * (1.0 + scale) + shift
    hb = h.astype(BF16)

    ua = jnp.dot(hb, wa_ref[...], preferred_element_type=F32)
    ub = jnp.dot(hb, wb_ref[...], preferred_element_type=F32)
    qn = _rms(ua[:, :Q_LORA], gq_ref[...]).astype(BF16)
    kvn = _rms(ua[:, Q_LORA:Q_LORA + KV_LORA], gkv_ref[...]).astype(BF16)
    kr = ua[:, Q_LORA + KV_LORA:]
    qf = jnp.dot(qn, wuq_ref[...], preferred_element_type=F32)
    kvf = jnp.dot(kvn, wukv_ref[...], preferred_element_type=F32)

    rc, rm, rp = rc_ref[...], rm_ref[...], rp_ref[...]

    def rope(t):
        return (t * rc + pltpu.roll(t, LANES - ROPE_HALF, 1) * rm + pltpu.roll(t, ROPE_HALF, 1) * rp)

    krr = rope(kr)
    qscale = (QK_NOPE + QK_ROPE) ** -0.5 * LOG2E
    for hh in range(MLA_HEADS):
        sl = slice(hh * HEAD_PAD, (hh + 1) * HEAD_PAD)
        q_ref[0, :, sl] = (rope(qf[:, sl]) * qscale).astype(BF16)
        k_ref[0, :, sl] = (kvf[:, sl] + krr).astype(BF16)
    ones_pad = (lax.broadcasted_iota(jnp.int32, (1, MLA_QK), 1) % HEAD_PAD >= V_DIM).astype(F32)
    v_ref[0] = (kvf[:, MLA_QK:] + ones_pad).astype(BF16)

    nq_ref[0] = (ub[:, :NA_WIDTH] * (NA_DIM ** -0.5 * LOG2E)).astype(BF16)
    nk_ref[0] = ub[:, NA_WIDTH:2 * NA_WIDTH].astype(BF16)
    nv_ref[0] = ub[:, 2 * NA_WIDTH:3 * NA_WIDTH].astype(BF16)
    gate_ref[0] = ub[:, 3 * NA_WIDTH:].astype(BF16)


def _in0(x, mod, g, wa, wb, gq, gkv, wuq, wukv, rc, rm, rp, tm=512):
    B, S, D = x.shape
    full = lambda a: pl.BlockSpec(a.shape, lambda b, i: (0,) * a.ndim)
    tok = lambda w: pl.BlockSpec((1, tm, w), lambda b, i: (b, i, 0))
    tab = pl.BlockSpec((tm, LANES), lambda b, i: (i, 0))
    widths = (MLA_QK, MLA_QK, MLA_QK, NA_WIDTH, NA_WIDTH, NA_WIDTH, MLA_WIDTH + NA_WIDTH)
    return pl.pallas_call(
        _in0_kernel,
        grid=(B, S // tm),
        in_specs=[tok(D), pl.BlockSpec((1, 3, D), lambda b, i: (b, 0, 0)), full(g), full(wa), full(wb),
                  full(gq), full(gkv), full(wuq), full(wukv), tab, tab, tab],
        out_specs=[tok(w) for w in widths],
        out_shape=[jax.ShapeDtypeStruct((B, S, w), BF16) for w in widths],
        compiler_params=_cparams(("parallel", "parallel")),
        name="in0",
    )(x, mod, g, wa, wb, gq, gkv, wuq, wukv, rc, rm, rp)


def _mla_kernel(q_ref, k_ref, v_ref, o_ref):
    low = lax.broadcasted_iota(jnp.int32, (q_ref.shape[1], HEAD_PAD), 1) < V_DIM

    def scores(hh):
        sl = slice(hh * HEAD_PAD, (hh + 1) * HEAD_PAD)
        return lax.dot_general(q_ref[0, :, sl], k_ref[0, :, sl], _NT, preferred_element_type=F32)

    s_next = scores(0)
    outs = []
    for hh in range(MLA_HEADS):
        s = s_next
        if hh + 1 < MLA_HEADS:
            s_next = scores(hh + 1)
        sl = slice(hh * HEAD_PAD, (hh + 1) * HEAD_PAD)
        p = jnp.exp2(s - jnp.max(s, axis=-1, keepdims=True))
        ol = jnp.dot(p.astype(BF16), v_ref[0, :, sl], preferred_element_type=F32)
        outs.append(ol / jnp.where(low, pltpu.roll(ol, V_DIM, 1), 1.0))
        if hh % 2 == 1:
            vs = slice((hh - 1) * V_DIM, (hh + 1) * V_DIM)
            o_ref[0, :, vs] = jnp.where(low, outs[hh - 1], pltpu.roll(outs[hh], V_DIM, 1)).astype(BF16)


def _mla(q, k, v, tq=1024):
    B, S, _ = q.shape
    return pl.pallas_call(
        _mla_kernel,
        grid=(B, S // tq),
        in_specs=[pl.BlockSpec((1, tq, MLA_QK), lambda b, i: (b, i, 0)),
                  pl.BlockSpec((1, S, MLA_QK), lambda b, i: (b, 0, 0)),
                  pl.BlockSpec((1, S, MLA_QK), lambda b, i: (b, 0, 0))],
        out_specs=pl.BlockSpec((1, tq, MLA_WIDTH), lambda b, i: (b, i, 0)),
        out_shape=jax.ShapeDtypeStruct((B, S, MLA_WIDTH), BF16),
        compiler_params=_cparams(("parallel", "arbitrary")),
        name="mla",
    )(q, k, v)


NA_GROUP = 4
NA_LANES = NA_GROUP * NA_DIM
NA_KEYS = NA_WIN_H * GRID_W
NA_UNROLL = 8


def _na_kernel(q_ref, k_ref, v_ref, bias_ref, o_ref, *, rows):
    rh = lax.broadcasted_iota(jnp.int32, (NA_LANES, NA_LANES), 0) // GRID_W
    lh = lax.broadcasted_iota(jnp.int32, (NA_LANES, NA_LANES), 1) // NA_DIM
    hmask = (rh == lh).astype(F32)

    def window(r):
        r0 = jnp.clip(r - NA_WIN_H // 2, 0, rows - NA_WIN_H)
        return r - r0, pl.ds(pl.multiple_of(r0 * GRID_W, GRID_W), NA_KEYS)

    def scores(r):
        d, win = window(r)
        qr = q_ref[0, pl.ds(pl.multiple_of(r * GRID_W, GRID_W), GRID_W), :].astype(F32)
        qs = (jnp.concatenate([qr] * NA_GROUP, axis=0) * hmask).astype(BF16)
        return lax.dot_general(qs, k_ref[0, win, :], _NT, preferred_element_type=F32) + bias_ref[0, d]

    def body(t, carry):
        base = t * NA_UNROLL
        s_next = scores(base)
        for i in range(NA_UNROLL):
            r = base + i
            s = s_next
            if i + 1 < NA_UNROLL:
                s_next = scores(r + 1)
            _, win = window(r)
            p = jnp.exp2(s - jnp.max(s, axis=-1, keepdims=True))
            l = jnp.sum(p, axis=-1, keepdims=True)
            o = jnp.dot(p.astype(BF16), v_ref[0, win, :], preferred_element_type=F32) / l * hmask
            out = o[0:GRID_W]
            for g in range(1, NA_GROUP):
                out = out + o[g * GRID_W:(g + 1) * GRID_W]
            o_ref[0, pl.ds(pl.multiple_of(r * GRID_W, GRID_W), GRID_W), :] = out.astype(BF16)
        return carry

    lax.fori_loop(0, rows // NA_UNROLL, body, 0)


def _natten(q, k, v, bias):
    B, S, _ = q.shape
    rows = S // GRID_W
    tok = pl.BlockSpec((1, S, NA_LANES), lambda g, b: (b, 0, g))
    return pl.pallas_call(
        functools.partial(_na_kernel, rows=rows),
        grid=(NA_HEADS // NA_GROUP, B),
        in_specs=[tok, tok, tok,
                  pl.BlockSpec((1, NA_WIN_H, NA_LANES, NA_KEYS), lambda g, b: (g, 0, 0, 0))],
        out_specs=tok,
        out_shape=jax.ShapeDtypeStruct((B, S, NA_WIDTH), BF16),
        compiler_params=_cparams(("parallel", "parallel")),
        name="natten",
    )(q, k, v, bias)


def _mid_kernel(om_ref, on_ref, gate_ref, x_ref, mod0_ref, mod1_ref, g_ref, wo_ref, wi_ref,
                x1_ref, xm_ref, z_ref):
    tm = x_ref.shape[1]
    halves = [slice(0, tm // 2), slice(tm // 2, tm)]

    def out_proj(rs):
        sg = _silu(gate_ref[0, rs, :].astype(F32))
        o = jnp.concatenate([om_ref[0, rs, :].astype(F32), on_ref[0, rs, :].astype(F32)], axis=-1) * sg
        return jnp.dot(o.astype(BF16), wo_ref[...], preferred_element_type=F32)

    outs = [out_proj(rs) for rs in halves]
    for rs, out in zip(halves, outs):
        x1 = x_ref[0, rs, :] + mod0_ref[0, 2:3, :] * out
        x1_ref[0, rs, :] = x1
        h = _rms(x1, g_ref[...]) * (1.0 + mod1_ref[0, 1:2, :]) + mod1_ref[0, 0:1, :]
        u = jnp.dot(h.astype(BF16), wi_ref[...], preferred_element_type=F32)
        xm_ref[0, rs, :] = u[:, :ML_INNER].astype(BF16)
        z_ref[0, rs, :] = u[:, ML_INNER:].astype(BF16)


def _mid(om, on, gate, x, mod0, mod1, g, wo, wi, tm=512):
    B, S, D = x.shape
    full = lambda a: pl.BlockSpec(a.shape, lambda b, i: (0,) * a.ndim)
    tok = lambda w: pl.BlockSpec((1, tm, w), lambda b, i: (b, i, 0))
    modspec = pl.BlockSpec((1, 3, D), lambda b, i: (b, 0, 0))
    return pl.pallas_call(
        _mid_kernel,
        grid=(B, S // tm),
        in_specs=[tok(MLA_WIDTH), tok(NA_WIDTH), tok(D), tok(D), modspec, modspec, full(g), full(wo), full(wi)],
        out_specs=[tok(D), tok(ML_INNER), tok(ML_INNER)],
        out_shape=[jax.ShapeDtypeStruct((B, S, D), F32),
                   jax.ShapeDtypeStruct((B, S, ML_INNER), BF16),
                   jax.ShapeDtypeStruct((B, S, ML_INNER), BF16)],
        compiler_params=_cparams(("parallel", "parallel")),
        name="mid",
    )(om, on, gate, x, mod0, mod1, g, wo, wi)


HALO = 16
SHIFT_ROWS = 128
SHIFT_WIN = SHIFT_ROWS + 2 * HALO
GATE_ROWS = 2 * ML_HEADS
PADR = 8
MXU_TAPS = (ML_CONV // 2 - 1, ML_CONV // 2 + 1)
VALU_TAPS = tuple(j for j in range(ML_CONV) if j not in MXU_TAPS)


def _log_sigmoid(x):
    return jnp.minimum(x, 0.0) - jnp.log1p(jnp.exp(-jnp.abs(x)))


def _chunk_scan(x, pos, op, ident, reverse):
    n = x.shape[0]
    k = 1
    while k < ML_L:
        if reverse:
            x = op(x, jnp.where(pos < ML_L - k, pltpu.roll(x, n - k, 0), ident))
        else:
            x = op(x, jnp.where(pos >= k, pltpu.roll(x, k, 0), ident))
        k *= 2
    return x


def _conv_kernel(xm_ref, xp_ref, xn_ref, sh_ref, cw_ref, cb_ref, bdqk_ref, bdv_ref, bg_ref,
                 q_ref, k_ref, v_ref, xc_ref, pre_ref, ext_ref, *, tm):
    i = pl.program_id(1)
    n = pl.num_programs(1)
    ext_ref[0:PADR, :] = jnp.where(i > 0, xp_ref[0].astype(F32)[HALO - PADR:, :], 0.0)
    ext_ref[PADR:PADR + tm, :] = xm_ref[0].astype(F32)
    ext_ref[PADR + tm:, :] = jnp.where(i < n - 1, xn_ref[0].astype(F32)[:PADR, :], 0.0)
    zero_halo = jnp.zeros((HALO, BD_TILE), BF16)
    qscale = ML_DIM ** -0.5
    pre = jnp.zeros((tm, LANES), F32) + bg_ref[...]

    def shifted(c):
        cs = slice(c * BD_TILE, (c + 1) * BD_TILE)
        prev = jnp.where(i > 0, xp_ref[0, :, cs], zero_halo)
        nxt = jnp.where(i < n - 1, xn_ref[0, :, cs], zero_halo)
        out = []
        for rb in range(tm // SHIFT_ROWS):
            lo, hi = rb * SHIFT_ROWS - HALO, (rb + 1) * SHIFT_ROWS + HALO
            parts = [prev] if lo < 0 else []
            parts.append(xm_ref[0, max(lo, 0):min(hi, tm), cs])
            if hi > tm:
                parts.append(nxt)
            win = jnp.concatenate(parts, axis=0) if len(parts) > 1 else parts[0]
            out.append(jnp.dot(sh_ref[...], win, preferred_element_type=F32))
        return out

    sh_next = shifted(0)
    for c in range(N_BD):
        cs = slice(c * BD_TILE, (c + 1) * BD_TILE)
        sh_cur = sh_next
        ys = []
        for rb in range(tm // SHIFT_ROWS):
            y = cb_ref[:, cs]
            for t, j in enumerate(MXU_TAPS):
                y = y + sh_cur[rb][t * SHIFT_ROWS:(t + 1) * SHIFT_ROWS] * cw_ref[j:j + 1, cs]
            for j in VALU_TAPS:
                r0 = PADR + rb * SHIFT_ROWS + j - ML_CONV // 2
                y = y + ext_ref[r0:r0 + SHIFT_ROWS, cs] * cw_ref[j:j + 1, cs]
            ys.append(y)
        xc = _silu(jnp.concatenate(ys, axis=0))
        xcb = xc.astype(BF16)
        xc_ref[0, :, cs] = xcb
        if c + 1 < N_BD:
            sh_next = shifted(c + 1)
        qk = jnp.dot(xcb, bdqk_ref[c], preferred_element_type=F32)
        vv = jnp.dot(xm_ref[0, :, cs], bdv_ref[c], preferred_element_type=F32)
        pre = pre + qk[:, 2 * BD_TILE:] + vv[:, BD_TILE:]
        q_ref[0, :, cs] = (qk[:, :BD_TILE] * qscale).astype(BF16)
        k_ref[0, :, cs] = qk[:, BD_TILE:2 * BD_TILE].astype(BF16)
        v_ref[0, :, cs] = vv[:, :BD_TILE].astype(BF16)

    pre_ref[0] = pre


def _gates_kernel(pre_ref, gc_ref, gr_ref):
    pre = pre_ref[0]
    lane = lax.broadcasted_iota(jnp.int32, pre.shape, 1)
    pos = lax.broadcasted_iota(jnp.int32, pre.shape, 0) % ML_L
    fwd = lane % 2 == 0
    lf = _log_sigmoid(pre)
    b = jnp.where(fwd, _chunk_scan(lf, pos, jnp.add, 0.0, False), _chunk_scan(lf, pos, jnp.add, 0.0, True))
    a = pre - pltpu.roll(b, LANES - GATE_ROWS, 1)
    c = jnp.where(fwd, _chunk_scan(a, pos, jnp.maximum, -jnp.inf, False),
                  _chunk_scan(a, pos, jnp.maximum, -jnp.inf, True))
    g = jnp.where(lane < GATE_ROWS, a, jnp.where(lane < 2 * GATE_ROWS, b, pltpu.roll(c, 2 * GATE_ROWS, 1)))
    gc_ref[0] = g
    gr_ref[0] = g.T[:3 * GATE_ROWS, :]


def _shift_matrix():
    m = np.zeros((len(MXU_TAPS) * SHIFT_ROWS, SHIFT_WIN), np.float32)
    r = np.arange(SHIFT_ROWS)
    for t, j in enumerate(MXU_TAPS):
        m[t * SHIFT_ROWS + r, r + HALO + j - ML_CONV // 2] = 1.0
    return jnp.asarray(m, dtype=BF16)


def _conv(xm, cw, cb, bdqk, bdv, bg, tm=512):
    sh = _shift_matrix()
    B, S, _ = xm.shape
    hb = tm // HALO
    full = lambda a: pl.BlockSpec(a.shape, lambda b, i: (0,) * a.ndim)
    tok = lambda w: pl.BlockSpec((1, tm, w), lambda b, i: (b, i, 0))
    return pl.pallas_call(
        functools.partial(_conv_kernel, tm=tm),
        grid=(B, S // tm),
        in_specs=[tok(ML_INNER),
                  pl.BlockSpec((1, HALO, ML_INNER), lambda b, i: (b, jnp.maximum(i * hb - 1, 0), 0)),
                  pl.BlockSpec((1, HALO, ML_INNER), lambda b, i: (b, jnp.minimum((i + 1) * hb, S // HALO - 1), 0)),
                  full(sh), full(cw), full(cb), full(bdqk), full(bdv), full(bg)],
        out_specs=[tok(ML_INNER), tok(ML_INNER), tok(ML_INNER), tok(ML_INNER), tok(LANES)],
        out_shape=[jax.ShapeDtypeStruct((B, S, ML_INNER), BF16)] * 4 + [jax.ShapeDtypeStruct((B, S, LANES), F32)],
        scratch_shapes=[pltpu.VMEM((tm + 2 * PADR, ML_INNER), F32)],
        compiler_params=_cparams(("parallel", "parallel")),
        name="conv",
    )(xm, xm, xm, sh, cw, cb, bdqk, bdv, bg)


def _gates(pre):
    B, S, _ = pre.shape
    return pl.pallas_call(
        _gates_kernel,
        grid=(B,),
        in_specs=[pl.BlockSpec((1, S, LANES), lambda b: (b, 0, 0))],
        out_specs=[pl.BlockSpec((1, S, LANES), lambda b: (b, 0, 0)),
                   pl.BlockSpec((1, 3 * GATE_ROWS, S), lambda b: (b, 0, 0))],
        out_shape=[jax.ShapeDtypeStruct((B, S, LANES), F32), jax.ShapeDtypeStruct((B, 3 * GATE_ROWS, S), F32)],
        compiler_params=_cparams(("parallel",)),
        name="gates",
    )(pre)


ML_UNROLL = 4
OUT_UNROLL = 4


def _mlstm_kernel(q_ref, k_ref, v_ref, gc_ref, gr_ref, gmh_ref, o_ref,
                  cf_ref, cb_ref, sf_ref, sb_ref, nst_ref, mf_ref, mb_ref, mrep_ref, brep_ref, *, seq):
    L = ML_L
    nc = seq // L
    head = pl.program_id(1)
    cf_ref[...] = jnp.zeros_like(cf_ref)
    cb_ref[...] = jnp.zeros_like(cb_ref)
    sf_ref[0] = jnp.zeros(sf_ref.shape[1:], BF16)
    sb_ref[nc] = jnp.zeros(sb_ref.shape[1:], BF16)
    nst_ref[...] = jnp.zeros_like(nst_ref)
    lane = lax.broadcasted_iota(jnp.int32, (L, LANES), 1)

    def wide(x, n):
        return jnp.concatenate([x] * n, axis=1)

    def state_step(c, d, ct_ref, st_ref, mst_ref, nvec, m):
        rows = pl.ds(pl.multiple_of(c * L, L), L)
        r = head * 2 + d
        kc, vc = k_ref[0, rows, :], v_ref[0, rows, :]
        gcol = gc_ref[0, rows, :]
        a_col, b_col, c_col = [jnp.sum(jnp.where(lane == g * GATE_ROWS + r, gcol, 0.0), axis=1, keepdims=True)
                               for g in range(3)]
        a_row = gr_ref[0, pl.ds(r, 1), rows]
        nst_ref[c, d:d + 1, :] = nvec
        mst_ref[c] = jnp.broadcast_to(m, mst_ref.shape[1:])
        mrep = jnp.maximum(c_col, m)
        mrep_ref[d, rows, :] = jnp.broadcast_to(mrep, (L, LANES))
        brep_ref[d, rows, :] = jnp.broadcast_to(b_col + mrep, (L, LANES))
        end = L - 1 if d == 0 else 0
        b_last, c_last = b_col[end:end + 1, :], c_col[end:end + 1, :]
        m_new = b_last + jnp.maximum(m, c_last)
        w_old = jnp.exp(b_last + m - m_new)
        w_s = jnp.exp(b_last + a_col - m_new)
        w_row = jnp.exp(b_last + a_row - m_new)
        wv = (w_s * vc.astype(F32)).astype(BF16)
        nxt = c + 1 if d == 0 else c
        for rb in range(ML_DIM // LANES):
            rs = slice(rb * LANES, (rb + 1) * LANES)
            blk = w_old * ct_ref[rs, :] + lax.dot_general(kc[:, rs], wv, _TN, preferred_element_type=F32)
            ct_ref[rs, :] = blk
            st_ref[nxt, rs, :] = blk.astype(BF16)
        dn = jnp.dot(jnp.broadcast_to(w_row, (8, L)).astype(BF16), kc, preferred_element_type=F32)
        return w_old * nvec + dn[0:1], m_new

    def scan_body(j, carry):
        nf, mf, nb, mb = carry
        nf, mf = state_step(j, 0, cf_ref, sf_ref, mf_ref, nf, mf)
        nb, mb = state_step(nc - 1 - j, 1, cb_ref, sb_ref, mb_ref, nb, mb)
        return nf, mf, nb, mb

    z_n = jnp.zeros((1, ML_DIM), F32)
    z_m = jnp.zeros((1, 1), F32)
    lax.fori_loop(0, nc, scan_body, (z_n, z_m, z_n, z_m), unroll=ML_UNROLL)

    ri = lax.broadcasted_iota(jnp.int32, (L, L), 0)
    ci = lax.broadcasted_iota(jnp.int32, (L, L), 1)

    def decay(d, rows, m, mask):
        a_row = gr_ref[0, pl.ds(head * 2 + d, 1), rows]
        mrep = mrep_ref[d, rows, :]
        dmat = jnp.exp(jnp.where(mask, a_row - wide(mrep, L // LANES), -jnp.inf))
        return dmat, jnp.exp(m - mrep), jnp.exp(-brep_ref[d, rows, :])

    def front(c):
        rows = pl.ds(pl.multiple_of(c * L, L), L)
        qc = q_ref[0, rows, :]
        kn = jnp.concatenate([k_ref[0, rows, :], nst_ref[c].astype(BF16)], axis=0)
        sx = lax.dot_general(qc, kn, _NT, preferred_element_type=F32)
        hf = jnp.dot(qc, sf_ref[c], preferred_element_type=F32)
        hb = jnp.dot(qc, sb_ref[c + 1], preferred_element_type=F32)
        return rows, sx, hf, hb

    def finish(c, rows, sx, hf, hb):
        d_f, wi_f, e_f = decay(0, rows, mf_ref[c][0:1, 0:1], ci <= ri)
        d_b, wi_b, e_b = decay(1, rows, mb_ref[c][0:1, 0:1], ci >= ri)
        s = sx[:, :L]
        s_f, s_b = s * d_f, s * d_b
        qn_f = jnp.broadcast_to(sx[:, L:L + 1], (L, LANES))
        qn_b = jnp.broadcast_to(sx[:, L + 1:L + 2], (L, LANES))
        r_f = 1.0 / jnp.maximum(jnp.abs(wi_f * qn_f + jnp.sum(s_f, axis=-1, keepdims=True)), e_f)
        r_b = 1.0 / jnp.maximum(jnp.abs(wi_b * qn_b + jnp.sum(s_b, axis=-1, keepdims=True)), e_b)
        p = (s_f * wide(r_f, L // LANES) + s_b * wide(r_b, L // LANES)).astype(BF16)
        hs = (hf * wide(wi_f * r_f, ML_DIM // LANES) + hb * wide(wi_b * r_b, ML_DIM // LANES)
              + jnp.dot(p, v_ref[0, rows, :], preferred_element_type=F32))
        mu = jnp.mean(hs, axis=-1, keepdims=True)
        var = jnp.mean(jnp.square(hs - mu), axis=-1, keepdims=True)
        o_ref[0, rows, :] = ((hs - mu) * lax.rsqrt(var + EPS) * gmh_ref[...]).astype(BF16)

    def out_body(t, carry):
        base = t * OUT_UNROLL
        nxt = front(base)
        for i in range(OUT_UNROLL):
            cur = nxt
            if i + 1 < OUT_UNROLL:
                nxt = front(base + i + 1)
            finish(base + i, *cur)
        return carry

    lax.fori_loop(0, nc // OUT_UNROLL, out_body, 0)


def _mlstm(q, k, v, gc, gr, gmh):
    B, S, _ = q.shape
    nc = S // ML_L
    tok = pl.BlockSpec((1, S, ML_DIM), lambda b, h: (b, 0, h))
    return pl.pallas_call(
        functools.partial(_mlstm_kernel, seq=S),
        grid=(B, ML_HEADS),
        in_specs=[tok, tok, tok,
                  pl.BlockSpec((1, S, LANES), lambda b, h: (b, 0, 0)),
                  pl.BlockSpec((1, 3 * GATE_ROWS, S), lambda b, h: (b, 0, 0)),
                  pl.BlockSpec((1, ML_DIM), lambda b, h: (0, h))],
        out_specs=tok,
        out_shape=jax.ShapeDtypeStruct((B, S, ML_INNER), BF16),
        scratch_shapes=[pltpu.VMEM((ML_DIM, ML_DIM), F32), pltpu.VMEM((ML_DIM, ML_DIM), F32),
                        pltpu.VMEM((nc + 1, ML_DIM, ML_DIM), BF16), pltpu.VMEM((nc + 1, ML_DIM, ML_DIM), BF16),
                        pltpu.VMEM((nc, 16, ML_DIM), F32),
                        pltpu.VMEM((nc, 8, LANES), F32), pltpu.VMEM((nc, 8, LANES), F32),
                        pltpu.VMEM((2, S, LANES), F32), pltpu.VMEM((2, S, LANES), F32)],
        compiler_params=_cparams(("parallel", "arbitrary")),
        name="mlstm",
    )(q, k, v, gc, gr, gmh)


def _out_kernel(hn_ref, xc_ref, z_ref, x1_ref, mod1_ref, skip_ref, wo_ref, gf_ref, y_ref):
    o = (hn_ref[0].astype(F32) + skip_ref[...] * xc_ref[0].astype(F32)) * _silu(z_ref[0].astype(F32))
    out = jnp.dot(o.astype(BF16), wo_ref[...], preferred_element_type=F32)
    x2 = x1_ref[0] + mod1_ref[0, 2:3, :] * out
    y_ref[0] = _rms(x2, gf_ref[...])


def _out(hn, xc, z, x1, mod1, skip, wo, gf, tm=512):
    B, S, D = x1.shape
    full = lambda a: pl.BlockSpec(a.shape, lambda b, i: (0,) * a.ndim)
    tok = lambda w: pl.BlockSpec((1, tm, w), lambda b, i: (b, i, 0))
    return pl.pallas_call(
        _out_kernel,
        grid=(B, S // tm),
        in_specs=[tok(ML_INNER), tok(ML_INNER), tok(ML_INNER), tok(D),
                  pl.BlockSpec((1, 3, D), lambda b, i: (b, 0, 0)), full(skip), full(wo), full(gf)],
        out_specs=tok(D),
        out_shape=jax.ShapeDtypeStruct((B, S, D), F32),
        compiler_params=_cparams(("parallel", "parallel")),
        name="out",
    )(hn, xc, z, x1, mod1, skip, wo, gf)


def _rope_tables(S):
    pos = jnp.arange(S, dtype=F32)
    inv = 1.0 / (ROPE_THETA ** (jnp.arange(0, QK_ROPE, 2, dtype=F32) / QK_ROPE))
    ang = pos[:, None] * inv[None, :]
    cos, sin = jnp.cos(ang), jnp.sin(ang)
    ones = jnp.ones((S, QK_NOPE), F32)
    z16 = jnp.zeros((S, ROPE_HALF), F32)
    z32 = jnp.zeros((S, HEAD_PAD - QK_NOPE - QK_ROPE), F32)
    z64 = jnp.zeros((S, QK_NOPE), F32)
    rc = jnp.concatenate([ones, cos, cos, z32], axis=-1)
    rm = jnp.concatenate([z64, -sin, z16, z32], axis=-1)
    rp = jnp.concatenate([z64, z16, sin, z32], axis=-1)
    return rc, rm, rp


def _na_bias(rpb):
    col = np.arange(GRID_W)
    col_start = np.clip(col - NA_WIN_W // 2, 0, GRID_W - NA_WIN_W)
    col_in = (col[None, :] >= col_start[:, None]) & (col[None, :] < col_start[:, None] + NA_WIN_W)
    period = 2 * GRID_W
    n_r = 2 * NA_WIN_H - 1
    vec = jnp.zeros((NA_HEADS, n_r, period), F32)
    vec = vec.at[..., :NA_WIN_W].set(rpb[..., NA_WIN_W - 1:])
    vec = vec.at[..., period - (NA_WIN_W - 1):].set(rpb[..., :NA_WIN_W - 1])
    toep = jnp.tile(vec, (1, 1, GRID_W))[..., :GRID_W * (period - 1)]
    toep = toep.reshape(NA_HEADS, n_r, GRID_W, period - 1)[..., :GRID_W]
    b = jnp.stack([toep[:, NA_WIN_H - 1 - d:2 * NA_WIN_H - 1 - d] for d in range(NA_WIN_H)], axis=1)
    b = jnp.where(col_in[None, None, None], b * LOG2E, NEG_BIG)
    b = b.transpose(0, 1, 3, 2, 4).reshape(NA_HEADS, NA_WIN_H, GRID_W, NA_KEYS)
    b = b.reshape(NA_HEADS // NA_GROUP, NA_GROUP, NA_WIN_H, GRID_W, NA_KEYS).transpose(0, 2, 1, 3, 4)
    return b.reshape(NA_HEADS // NA_GROUP, NA_WIN_H, NA_LANES, NA_KEYS).astype(F32)


def _block_diag(w):
    per = BD_TILE // QKV_BLOCK
    wt = w.transpose(0, 2, 1).reshape(N_BD, per, QKV_BLOCK, QKV_BLOCK)
    eye = jnp.eye(per, dtype=w.dtype)
    return jnp.einsum('caio,ab->caibo', wt, eye).reshape(N_BD, BD_TILE, BD_TILE)


def _prep(g_norm, g_final, w_in0, g_qlat, g_kvlat, w_uq, w_ukv, na_rpb, w_out0,
          w_in1, conv_w, conv_b, w_q, w_k, w_v, w_gate, b_gate, g_mh, skip, w_out1):
    p = {}
    w0 = w_in0[0]
    o = np.cumsum((Q_LORA, KV_LORA, QK_ROPE, NA_WIDTH, NA_WIDTH, NA_WIDTH))
    kr = jnp.zeros((D_MODEL, HEAD_PAD), F32).at[:, QK_NOPE:QK_NOPE + QK_ROPE].set(w0[:, o[1]:o[2]])
    p['wa'] = jnp.concatenate([w0[:, :o[1]], kr], axis=-1).astype(BF16)
    p['wb'] = w0[:, o[2]:].astype(BF16)
    wq = w_uq[0].reshape(Q_LORA, MLA_HEADS, QK_NOPE + QK_ROPE)
    p['wuq'] = jnp.pad(wq, ((0, 0), (0, 0), (0, HEAD_PAD - QK_NOPE - QK_ROPE))).reshape(Q_LORA, MLA_QK).astype(BF16)
    wkv = w_ukv[0].reshape(KV_LORA, MLA_HEADS, QK_NOPE + V_DIM)
    wk = jnp.pad(wkv[:, :, :QK_NOPE], ((0, 0), (0, 0), (0, HEAD_PAD - QK_NOPE))).reshape(KV_LORA, MLA_QK)
    wv = jnp.pad(wkv[:, :, QK_NOPE:], ((0, 0), (0, 0), (0, HEAD_PAD - V_DIM))).reshape(KV_LORA, MLA_QK)
    p['wukv'] = jnp.concatenate([wk, wv], axis=-1).astype(BF16)
    p['gq'] = g_qlat[0].reshape(1, Q_LORA)
    p['gkv'] = g_kvlat[0].reshape(1, KV_LORA)
    p['g0'] = g_norm[0].reshape(1, D_MODEL)
    p['g1'] = g_norm[1].reshape(1, D_MODEL)
    p['gf'] = g_final.reshape(1, D_MODEL)
    p['bias'] = _na_bias(na_rpb[0])
    p['wo0'] = w_out0[0].astype(BF16)
    p['wi1'] = w_in1[0].astype(BF16)
    p['cw'] = conv_w[0]
    p['cb'] = conv_b[0].reshape(1, ML_INNER)
    gate_perm = np.array([d * 2 * ML_HEADS + h for h in range(ML_HEADS) for d in range(2)]
                         + [d * 2 * ML_HEADS + ML_HEADS + h for h in range(ML_HEADS) for d in range(2)])
    wg = w_gate[0][:, gate_perm].reshape(3, ML_INNER // QKV_BLOCK, QKV_BLOCK, 2 * GATE_ROWS)

    def fold(w, g):
        f = jnp.einsum('noi,nog->nig', w, g, precision=lax.Precision.HIGHEST).reshape(ML_INNER, 2 * GATE_ROWS)
        return jnp.pad(f, ((0, 0), (0, LANES - 2 * GATE_ROWS))).reshape(N_BD, BD_TILE, LANES)

    p['bdqk'] = jnp.concatenate([_block_diag(w_q[0]), _block_diag(w_k[0]),
                                 fold(w_q[0], wg[0]) + fold(w_k[0], wg[1])], axis=-1).astype(BF16)
    p['bdv'] = jnp.concatenate([_block_diag(w_v[0]), fold(w_v[0], wg[2])], axis=-1).astype(BF16)
    p['bg'] = jnp.pad(b_gate[0][gate_perm].reshape(1, -1), ((0, 0), (0, LANES - 2 * GATE_ROWS)))
    p['gmh'] = g_mh[0].reshape(1, ML_INNER)
    p['skip'] = skip[0].reshape(1, ML_INNER)
    p['wo1'] = w_out1[0].astype(BF16)
    return p


def _trunk(x, mod0, mod1, p, tabs, tm=512):
    B, S, _ = x.shape
    rc, rm, rp = tabs
    q, k, v, nq, nk, nv, gate = _in0(x, mod0, p['g0'], p['wa'], p['wb'], p['gq'], p['gkv'],
                                      p['wuq'], p['wukv'], rc, rm, rp, tm=tm)
    o_mla = _mla(q, k, v)
    o_na = _natten(nq, nk, nv, p['bias'])
    x1, xm, z = _mid(o_mla, o_na, gate, x, mod0, mod1, p['g1'], p['wo0'], p['wi1'], tm=tm)
    mq, mk, mv, xc, pre = _conv(xm, p['cw'], p['cb'], p['bdqk'], p['bdv'], p['bg'], tm=tm)
    gc, gr = _gates(pre)
    hn = _mlstm(mq, mk, mv, gc, gr, p['gmh'])
    return _out(hn, xc, z, x1, mod1, p['skip'], p['wo1'], p['gf'], tm=tm)


def kernel(x_prompt, x_sample, c_prompt, c_sample, g_norm, w_ada, b_ada, g_final, w_in0, g_qlat, g_kvlat, w_uq, w_ukv, na_rpb, w_out0, w_in1, conv_w, conv_b, w_q, w_k, w_v, w_gate, b_gate, g_mh, skip, w_out1):
    p = _prep(g_norm, g_final, w_in0, g_qlat, g_kvlat, w_uq, w_ukv, na_rpb, w_out0,
              w_in1, conv_w, conv_b, w_q, w_k, w_v, w_gate, b_gate, g_mh, skip, w_out1)
    nb_p = x_prompt.shape[0]
    mod = _ada(jnp.concatenate([c_prompt, c_sample], axis=0), w_ada, b_ada)
    mod = mod.reshape(DEPTH, -1, 3, D_MODEL)
    outs = []
    for x, sl in ((x_prompt, slice(0, nb_p)), (x_sample, slice(nb_p, None))):
        tabs = _rope_tables(x.shape[1])
        outs.append(_trunk(x, mod[0, sl], mod[1, sl], p, tabs))
    return tuple(outs)
```
